```python
import math
import jax, jax.numpy as jnp
from jax import lax
import numpy as np

D_MODEL = 1024
BATCH = 4
SEQ = 8192
DEPTH = 4

CHUNK = 64
N_MIXERS = 4
EXPAND = 2
D_INNER = EXPAND * D_MODEL
EPS = 1e-6
CONV_K = 31
FOX_HEAD_DIM = 128
FOX_HEADS = D_INNER // FOX_HEAD_DIM
QUERY_BLOCK = 128
GLA_HEADS = 4
GLA_KEY_DIM = D_INNER // 2
GLA_DK = GLA_KEY_DIM // GLA_HEADS
GLA_DV = D_INNER // GLA_HEADS
GLA_GATE_RANK = 16
GLA_GATE_NORM = 16.0
LRU_CONV_K = 4
LRU_BLOCKS = 16
LRU_BLOCK_W = D_INNER // LRU_BLOCKS
LRU_C = 8.0

kernel_name = "hybrid_interleaved_conv_fox_gla_rglru"


def rmsnorm(x, g):
    xf = x.astype(jnp.float32)
    y = xf * lax.rsqrt(jnp.mean(xf * xf, axis=-1, keepdims=True) + EPS)
    return (y * g.astype(jnp.float32)).astype(x.dtype)


def layernorm(x, g, b):
    xf = x.astype(jnp.float32)
    mu = jnp.mean(xf, axis=-1, keepdims=True)
    var = jnp.mean(jnp.square(xf - mu), axis=-1, keepdims=True)
    y = (xf - mu) * lax.rsqrt(var + EPS) * g.astype(jnp.float32) + b.astype(jnp.float32)
    return y.astype(x.dtype)


def causal_dwconv(x, w):
    K, C = w.shape
    return lax.conv_general_dilated(
        x, w[:, None, :].astype(x.dtype), window_strides=(1,), padding=[(K - 1, 0)],
        dimension_numbers=('NWC', 'WIO', 'NWC'), feature_group_count=C)


def conformer_conv_mixer(h, w_in, w_dw, b_dw, ln_g, ln_b, w_out):
    u = h @ w_in
    a, b, gate = jnp.split(u, 3, axis=-1)
    v = a * jax.nn.sigmoid(b)
    v = causal_dwconv(v, w_dw) + b_dw
    v = layernorm(v, ln_g, ln_b)
    return (jax.nn.silu(v) * jax.nn.silu(gate)) @ w_out


def fox_mixer(h, w_in, b_f, w_out):
    Bsz, S, _ = h.shape
    u = h @ w_in
    q, k, v, f_logit, gate = jnp.split(
        u, [D_INNER, 2 * D_INNER, 3 * D_INNER, 3 * D_INNER + FOX_HEADS], axis=-1)

    def heads(t):
        return t.reshape(Bsz, S, FOX_HEADS, FOX_HEAD_DIM).transpose(0, 2, 1, 3)

    q, k, v = heads(q), heads(k), heads(v)
    log_f = jax.nn.log_sigmoid(f_logit.astype(jnp.float32) + b_f.astype(jnp.float32))
    cum = jnp.cumsum(log_f, axis=1).transpose(0, 2, 1)
    n_blk = S // QUERY_BLOCK
    q_blk = q.reshape(Bsz, FOX_HEADS, n_blk, QUERY_BLOCK, FOX_HEAD_DIM).transpose(2, 0, 1, 3, 4)
    c_blk = cum.reshape(Bsz, FOX_HEADS, n_blk, QUERY_BLOCK).transpose(2, 0, 1, 3)
    starts = jnp.arange(n_blk, dtype=jnp.int32) * QUERY_BLOCK
    k_pos = jnp.arange(S, dtype=jnp.int32)
    scale = FOX_HEAD_DIM ** -0.5

    def attend(args):
        q_i, c_i, start = args
        logits = jnp.einsum('bhqd,bhkd->bhqk', q_i, k).astype(jnp.float32) * scale
        logits = logits + (c_i[..., None] - cum[:, :, None, :])
        q_pos = start + jnp.arange(QUERY_BLOCK, dtype=jnp.int32)
        mask = k_pos[None, :] <= q_pos[:, None]
        logits = jnp.where(mask, logits, -jnp.inf)
        p = jax.nn.softmax(logits, axis=-1)
        return jnp.einsum('bhqk,bhkd->bhqd', p.astype(v.dtype), v)

    o = lax.map(attend, (q_blk, c_blk, starts))
    o = o.transpose(1, 0, 3, 2, 4).reshape(Bsz, S, D_INNER)
    return (o * jax.nn.silu(gate)) @ w_out


def gla_mixer(h, w_in, w_g1, w_g2, b_g, gn_g, w_out):
    Bsz, S, _ = h.shape
    u = h @ w_in
    q, k, v, gate = jnp.split(u, [GLA_KEY_DIM, 2 * GLA_KEY_DIM, 2 * GLA_KEY_DIM + D_INNER], axis=-1)
    g = jax.nn.log_sigmoid(((h @ w_g1) @ w_g2 + b_g).astype(jnp.float32)) / GLA_GATE_NORM
    n_c = S // CHUNK

    def chunks(t, d):
        return t.astype(jnp.float32).reshape(Bsz, n_c, CHUNK, GLA_HEADS, d).transpose(1, 0, 3, 2, 4)

    qc = chunks(q, GLA_DK) * (GLA_DK ** -0.5)
    kc = chunks(k, GLA_DK)
    vc = chunks(v, GLA_DV)
    gc = chunks(g, GLA_DK)
    tril = jnp.tril(jnp.ones((CHUNK, CHUNK), dtype=bool))

    def step(state, xs):
        q_c, k_c, v_c, g_c = xs
        b = jnp.cumsum(g_c, axis=2)
        b_last = b[:, :, -1, :]
        b_mid = b[:, :, CHUNK // 2:CHUNK // 2 + 1, :]
        o_inter = jnp.einsum('bhtd,bhde->bhte', q_c * jnp.exp(b), state)
        scores = jnp.einsum('bhtd,bhsd->bhts', q_c * jnp.exp(b - b_mid), k_c * jnp.exp(b_mid - b))
        scores = jnp.where(tril, scores, 0.0)
        o_intra = jnp.einsum('bhts,bhse->bhte', scores, v_c)
        k_dec = k_c * jnp.exp(b_last[:, :, None, :] - b)
        state = jnp.exp(b_last)[..., None] * state + jnp.einsum('bhsd,bhse->bhde', k_dec, v_c)
        return state, o_inter + o_intra

    state0 = jnp.zeros((Bsz, GLA_HEADS, GLA_DK, GLA_DV), jnp.float32)
    _, o = lax.scan(step, state0, (qc, kc, vc, gc))
    o = o.transpose(1, 0, 3, 2, 4).reshape(Bsz, S, GLA_HEADS, GLA_DV)
    o = rmsnorm(o, gn_g).reshape(Bsz, S, D_INNER)
    return (o.astype(h.dtype) * jax.nn.silu(gate)) @ w_out


def rglru_mixer(h, w_in, w_conv, b_conv, w_a, b_a, w_x, b_x, lam, w_out):
    Bsz, S, _ = h.shape
    u = h @ w_in
    xb, gate = jnp.split(u, 2, axis=-1)
    xb = causal_dwconv(xb, w_conv) + b_conv
    xblk = xb.reshape(Bsz, S, LRU_BLOCKS, LRU_BLOCK_W)
    r = jax.nn.sigmoid(jnp.einsum('bsnd,nde->bsne', xblk, w_a).reshape(Bsz, S, D_INNER) + b_a)
    i = jax.nn.sigmoid(jnp.einsum('bsnd,nde->bsne', xblk, w_x).reshape(Bsz, S, D_INNER) + b_x)
    log_a = (-LRU_C * r.astype(jnp.float32)) * jax.nn.softplus(-lam.astype(jnp.float32))
    a = jnp.exp(log_a)
    beta = jnp.sqrt(-jnp.expm1(2.0 * log_a))
    bterm = beta * (i * xb).astype(jnp.float32)

    def combine(left, right):
        a_l, b_l = left
        a_r, b_r = right
        return a_l * a_r, a_r * b_l + b_r

    _, hs = lax.associative_scan(combine, (a, bterm), axis=1)
    return (hs.astype(h.dtype) * jax.nn.silu(gate)) @ w_out


def _count(m):
    return len(range(m, DEPTH, N_MIXERS))


def setup_inputs(seed: int = 0) -> dict:
    key = jax.random.key(seed)
    ks = iter(jax.random.split(key, 40))
    f32 = jnp.float32

    def nrm(shape, scale):
        return jax.random.normal(next(ks), shape, f32) * scale

    def gain(shape):
        return 1.0 + nrm(shape, 0.01)

    n0, n1, n2, n3 = _count(0), _count(1), _count(2), _count(3)
    E, D = D_INNER, D_MODEL
    a0 = jax.random.uniform(next(ks), (n3, E), f32, 0.9, 0.999)
    s = a0 ** (1.0 / LRU_C)
    lam = jnp.log(s) - jnp.log1p(-s)
    return {
        "x": nrm((BATCH, SEQ, D), 1.0),
        "norm_g": gain((DEPTH, D)),
        "final_g": gain((D,)),
        "conv_w_in": nrm((n0, D, 3 * E), D ** -0.5),
        "conv_w_dw": nrm((n0, CONV_K, E), CONV_K ** -0.5),
        "conv_b_dw": nrm((n0, E), 0.02),
        "conv_ln_g": gain((n0, E)),
        "conv_ln_b": nrm((n0, E), 0.02),
        "conv_w_out": nrm((n0, E, D), E ** -0.5),
        "fox_w_in": nrm((n1, D, 4 * E + FOX_HEADS), D ** -0.5),
        "fox_b_f": 2.0 + nrm((n1, FOX_HEADS), 0.5),
        "fox_w_out": nrm((n1, E, D), E ** -0.5),
        "gla_w_in": nrm((n2, D, 2 * GLA_KEY_DIM + 2 * E), D ** -0.5),
        "gla_w_g1": nrm((n2, D, GLA_GATE_RANK), D ** -0.5),
        "gla_w_g2": nrm((n2, GLA_GATE_RANK, GLA_KEY_DIM), GLA_GATE_RANK ** -0.5),
        "gla_b_g": nrm((n2, GLA_KEY_DIM), 0.02),
        "gla_gn_g": gain((n2, GLA_DV)),
        "gla_w_out": nrm((n2, E, D), E ** -0.5),
        "lru_w_in": nrm((n3, D, 2 * E), D ** -0.5),
        "lru_w_conv": nrm((n3, LRU_CONV_K, E), LRU_CONV_K ** -0.5),
        "lru_b_conv": nrm((n3, E), 0.02),
        "lru_w_a": nrm((n3, LRU_BLOCKS, LRU_BLOCK_W, LRU_BLOCK_W), LRU_BLOCK_W ** -0.5),
        "lru_b_a": nrm((n3, E), 0.02),
        "lru_w_x": nrm((n3, LRU_BLOCKS, LRU_BLOCK_W, LRU_BLOCK_W), LRU_BLOCK_W ** -0.5),
        "lru_b_x": nrm((n3, E), 0.02),
        "lru_lam": lam,
        "lru_w_out": nrm((n3, E, D), E ** -0.5),
    }


def reference(x, norm_g, final_g,
              conv_w_in, conv_w_dw, conv_b_dw, conv_ln_g, conv_ln_b, conv_w_out,
              fox_w_in, fox_b_f, fox_w_out,
              gla_w_in, gla_w_g1, gla_w_g2, gla_b_g, gla_gn_g, gla_w_out,
              lru_w_in, lru_w_conv, lru_b_conv, lru_w_a, lru_b_a, lru_w_x, lru_b_x, lru_lam, lru_w_out):
    for i in range(DEPTH):
        m, j = i % N_MIXERS, i // N_MIXERS
        hn = rmsnorm(x, norm_g[i])
        if m == 0:
            out = conformer_conv_mixer(hn, conv_w_in[j], conv_w_dw[j], conv_b_dw[j],
                                       conv_ln_g[j], conv_ln_b[j], conv_w_out[j])
        elif m == 1:
            out = fox_mixer(hn, fox_w_in[j], fox_b_f[j], fox_w_out[j])
        elif m == 2:
            out = gla_mixer(hn, gla_w_in[j], gla_w_g1[j], gla_w_g2[j], gla_b_g[j],
                            gla_gn_g[j], gla_w_out[j])
        else:
            out = rglru_mixer(hn, lru_w_in[j], lru_w_conv[j], lru_b_conv[j], lru_w_a[j],
                              lru_b_a[j], lru_w_x[j], lru_b_x[j], lru_lam[j], lru_w_out[j])
        x = x + out.astype(x.dtype)
    return rmsnorm(x, final_g)
```

```python
import functools

import jax
import jax.numpy as jnp
from jax import lax
from jax.experimental import pallas as pl
from jax.experimental.pallas import tpu as pltpu

F32 = jnp.float32
BF16 = jnp.bfloat16
EPS = 1e-6

V7X_VMEM_BYTES = 64 * 1024 * 1024
VMEM_LIMIT_BYTES = V7X_VMEM_BYTES - 16 * 1024 * 1024
LANES = 128

CONV_K = 31
CONV_HALO = 32
FOX_HEAD_DIM = 128
GLA_HEADS = 4
GLA_CHUNK = 64
GLA_GATE_NORM = 16.0
LRU_CONV_K = 4
LRU_HALO = 16
LRU_BLOCK_W = 128
LRU_C = 8.0


def _params(*sem):
    return pltpu.CompilerParams(dimension_semantics=sem, vmem_limit_bytes=VMEM_LIMIT_BYTES)


def _sigmoid(x):
    return 1.0 / (1.0 + jnp.exp(-x))


def _silu(x):
    return x * _sigmoid(x)


def _log_sigmoid(x):
    return jnp.minimum(x, 0.0) - jnp.log(1.0 + jnp.exp(-jnp.abs(x)))


def _rms(x, g):
    return x * lax.rsqrt(jnp.mean(x * x, axis=-1, keepdims=True) + EPS) * g


def _dot(a, b):
    return jnp.dot(a, b, preferred_element_type=F32)


def _dot_nt(a, b):
    return lax.dot_general(a, b, (((1,), (1,)), ((), ())), preferred_element_type=F32)


def _dot_tn(a, b):
    return lax.dot_general(a, b, (((0,), (0,)), ((), ())), preferred_element_type=F32)


def _dot_01(m01, x):
    x1 = x.astype(BF16)
    r1 = x - x1.astype(F32)
    x2 = r1.astype(BF16)
    x3 = (r1 - x2.astype(F32)).astype(BF16)
    return _dot(m01, x1) + _dot(m01, x2) + _dot(m01, x3)


def _rms_matmul_body(x_ref, g_ref, w_ref, o_ref, hn_ref):
    @pl.when(pl.program_id(1) == 0)
    def _():
        hn_ref[...] = _rms(x_ref[...], g_ref[...]).astype(BF16)

    o_ref[...] = _dot(hn_ref[...], w_ref[...]).astype(o_ref.dtype)


def _rms_matmul(x, g, w, *, tm=1024, tn=1024):
    T, D = x.shape
    N = w.shape[1]
    assert T % tm == 0 and N % tn == 0
    return pl.pallas_call(
        _rms_matmul_body,
        grid=(T // tm, N // tn),
        in_specs=[
            pl.BlockSpec((tm, D), lambda i, j: (i, 0)),
            pl.BlockSpec((1, D), lambda i, j: (0, 0)),
            pl.BlockSpec((D, tn), lambda i, j: (0, j)),
        ],
        out_specs=pl.BlockSpec((tm, tn), lambda i, j: (i, j)),
        out_shape=jax.ShapeDtypeStruct((T, N), BF16),
        scratch_shapes=[pltpu.VMEM((tm, D), BF16)],
        compiler_params=_params("parallel", "arbitrary"),
        name="rms_in_proj",
    )(x, g, w)


def _gated_out_body(o_ref, gate_ref, w_ref, x_ref, y_ref):
    og = (o_ref[...].astype(F32) * _silu(gate_ref[...].astype(F32))).astype(BF16)
    y_ref[...] = x_ref[...] + _dot(og, w_ref[...])


def _gated_out(o, u, gate_blk, w_out, x, *, tm=512):
    T, E = o.shape
    D = x.shape[1]
    assert T % tm == 0
    return pl.pallas_call(
        _gated_out_body,
        grid=(T // tm,),
        in_specs=[
            pl.BlockSpec((tm, E), lambda i: (i, 0)),
            pl.BlockSpec((tm, E), lambda i: (i, gate_blk)),
            pl.BlockSpec((E, D), lambda i: (0, 0)),
            pl.BlockSpec((tm, D), lambda i: (i, 0)),
        ],
        out_specs=pl.BlockSpec((tm, D), lambda i: (i, 0)),
        out_shape=jax.ShapeDtypeStruct((T, D), F32),
        compiler_params=_params("parallel"),
        name="gated_out_proj",
    )(o, u, w_out, x)


def _conv_body(a_ref, b_ref, gate_ref, ah_ref, bh_ref, x_ref, wdw_ref, bdw_ref, lng_ref, lnb_ref,
               wout_ref, y_ref, buf_ref, acc_ref, *, tm, E, rows, cols):
    i = pl.program_id(1)
    v = a_ref[...].astype(F32) * _sigmoid(b_ref[...].astype(F32))
    vh = ah_ref[...].astype(F32) * _sigmoid(bh_ref[...].astype(F32))
    buf_ref[0:CONV_HALO, :] = jnp.where(i > 0, vh, 0.0)
    buf_ref[CONV_HALO:, :] = v

    off = CONV_HALO - (CONV_K - 1)
    for r0 in range(0, tm, rows):
        for c0 in range(0, E, cols):
            acc = jnp.broadcast_to(bdw_ref[:, c0:c0 + cols], (rows, cols))
            for k in range(CONV_K):
                acc = acc + wdw_ref[k:k + 1, c0:c0 + cols] * buf_ref[r0 + off + k:r0 + off + k + rows, c0:c0 + cols]
            acc_ref[r0:r0 + rows, c0:c0 + cols] = acc

    c = acc_ref[...]
    mu = jnp.mean(c, axis=-1, keepdims=True)
    d = c - mu
    var = jnp.mean(d * d, axis=-1, keepdims=True)
    ln = d * lax.rsqrt(var + EPS) * lng_ref[...] + lnb_ref[...]
    og = (_silu(ln) * _silu(gate_ref[...].astype(F32))).astype(BF16)
    y_ref[...] = x_ref[...] + _dot(og, wout_ref[...])


def _conv_layer(x, g, w_in, w_dw, b_dw, ln_g, ln_b, w_out, *, B, S, tm=256):
    T, D = x.shape
    E = w_out.shape[0]
    u = _rms_matmul(x, g, w_in)
    nS = S // tm
    hpb = tm // CONV_HALO

    def tile(col):
        return pl.BlockSpec((tm, E), lambda b, i: (b * nS + i, col))

    def halo(col):
        return pl.BlockSpec((CONV_HALO, E), lambda b, i: (jnp.maximum((b * nS + i) * hpb - 1, 0), col))

    def row(n):
        return pl.BlockSpec((1, n), lambda b, i: (0, 0))

    body = functools.partial(_conv_body, tm=tm, E=E, rows=32, cols=512)
    return pl.pallas_call(
        body,
        grid=(B, nS),
        in_specs=[tile(0), tile(1), tile(2), halo(0), halo(1),
                  pl.BlockSpec((tm, D), lambda b, i: (b * nS + i, 0)),
                  pl.BlockSpec((CONV_K, E), lambda b, i: (0, 0)),
                  row(E), row(E), row(E),
                  pl.BlockSpec((E, D), lambda b, i: (0, 0))],
        out_specs=pl.BlockSpec((tm, D), lambda b, i: (b * nS + i, 0)),
        out_shape=jax.ShapeDtypeStruct((T, D), F32),
        scratch_shapes=[pltpu.VMEM((tm + CONV_HALO, E), F32), pltpu.VMEM((tm, E), F32)],
        compiler_params=_params("parallel", "parallel"),
        name="conv_mixer",
    )(u, u, u, u, u, x, w_dw, b_dw, ln_g, ln_b, w_out)


def _fox_cum_body(x_ref, g_ref, wf_ref, bf_ref, o_ref, carry_ref, *, tc, H):
    @pl.when(pl.program_id(1) == 0)
    def _():
        carry_ref[...] = jnp.zeros_like(carry_ref)

    hn = _rms(x_ref[...], g_ref[...]).astype(BF16)
    log_f = _log_sigmoid(_dot(hn, wf_ref[...]) + bf_ref[...])
    r = lax.broadcasted_iota(jnp.int32, (tc, tc), 0)
    c = lax.broadcasted_iota(jnp.int32, (tc, tc), 1)
    tril = jnp.where(c <= r, 1.0, 0.0).astype(BF16)
    cum = _dot_01(tril, log_f) + carry_ref[...]
    carry_ref[...] = cum[tc - 1:tc, :]
    o_ref[0] = cum.T[0:H, :]


def _fox_cum(x, g, wf, bfp, *, B, S, H, tc=512):
    T, D = x.shape
    nS = S // tc
    body = functools.partial(_fox_cum_body, tc=tc, H=H)
    return pl.pallas_call(
        body,
        grid=(B, nS),
        in_specs=[
            pl.BlockSpec((tc, D), lambda b, i: (b * nS + i, 0)),
            pl.BlockSpec((1, D), lambda b, i: (0, 0)),
            pl.BlockSpec((D, LANES), lambda b, i: (0, 0)),
            pl.BlockSpec((1, LANES), lambda b, i: (0, 0)),
        ],
        out_specs=pl.BlockSpec((1, H, tc), lambda b, i: (b, 0, i)),
        out_shape=jax.ShapeDtypeStruct((B, H, S), F32),
        scratch_shapes=[pltpu.VMEM((1, LANES), F32)],
        compiler_params=_params("parallel", "arbitrary"),
        name="fox_forget_cumsum",
    )(x, g, wf, bfp)


def _fox_attn_body(q_ref, k_ref, v_ref, c_ref, o_ref, *, t, scale):
    i = pl.program_id(2)
    q = (q_ref[...].astype(F32) * scale).astype(BF16)
    c0 = c_ref[0, 0, pl.ds(i, 1), :][:, 0:1]

    def block(j, carry, masked):
        m, l, acc = carry
        start = pl.multiple_of(j * t, t)
        k = k_ref[pl.ds(start, t), :]
        v = v_ref[pl.ds(start, t), :]
        s = _dot_nt(q, k) + (c0 - c_ref[0, 0, pl.ds(j, 1), :])
        if masked:
            r = lax.broadcasted_iota(jnp.int32, (t, t), 0)
            c = lax.broadcasted_iota(jnp.int32, (t, t), 1)
            s = jnp.where(c <= r, s, -jnp.inf)
        m_new = jnp.maximum(m, jnp.max(s, axis=-1, keepdims=True))
        alpha = jnp.exp(m - m_new)
        p = jnp.exp(s - m_new)
        l = alpha * l + jnp.sum(p, axis=-1, keepdims=True)
        acc = alpha * acc + _dot(p.astype(BF16), v)
        return m_new, l, acc

    dh = q_ref.shape[1]
    init = (jnp.full((t, 1), -jnp.inf, F32), jnp.zeros((t, 1), F32), jnp.zeros((t, dh), F32))
    carry = lax.fori_loop(0, i, lambda j, cr: block(j, cr, False), init)
    _, l, acc = block(i, carry, True)
    o_ref[...] = (acc / l).astype(o_ref.dtype)


def _fox_attention(u, cum, *, B, S, H, E, t=512):
    T = u.shape[0]
    dh = FOX_HEAD_DIM
    n = S // t
    cum4 = cum.reshape(B, H, n, t)
    body = functools.partial(_fox_attn_body, t=t, scale=dh ** -0.5)
    return pl.pallas_call(
        body,
        grid=(B, H, n),
        in_specs=[
            pl.BlockSpec((t, dh), lambda b, h, i: (b * n + i, h)),
            pl.BlockSpec((S, dh), lambda b, h, i: (b, H + h)),
            pl.BlockSpec((S, dh), lambda b, h, i: (b, 2 * H + h)),
            pl.BlockSpec((1, 1, n, t), lambda b, h, i: (b, h, 0, 0)),
        ],
        out_specs=pl.BlockSpec((t, dh), lambda b, h, i: (b * n + i, h)),
        out_shape=jax.ShapeDtypeStruct((T, E), BF16),
        compiler_params=_params("parallel", "parallel", "parallel"),
        name="fox_attention",
    )(u, u, u, cum4)


def _fox_layer(x, g, w_in, b_f, w_out, *, B, S):
    E = w_out.shape[0]
    H = E // FOX_HEAD_DIM
    w_main = w_in[:, :3 * E]
    w_f = w_in[:, 3 * E:3 * E + H]
    w_gate = w_in[:, 3 * E + H:]
    w_cat = jnp.concatenate([w_main, w_gate], axis=1).astype(BF16)
    wf_pad = jnp.pad(w_f, ((0, 0), (0, LANES - H))).astype(BF16)
    bf_pad = jnp.pad(b_f, (0, LANES - H)).reshape(1, LANES)
    u = _rms_matmul(x, g, w_cat)
    cum = _fox_cum(x, g, wf_pad, bf_pad, B=B, S=S, H=H)
    o = _fox_attention(u, cum, B=B, S=S, H=H, E=E)
    return _gated_out(o, u, 3, w_out.astype(BF16), x)


def _gla_gate_body(x_ref, g_ref, w1_ref, w2_ref, b_ref, o_ref):
    hn = _rms(x_ref[...], g_ref[...]).astype(BF16)
    low = _dot(hn, w1_ref[...]).astype(BF16)
    o_ref[...] = _log_sigmoid(_dot(low, w2_ref[...]) + b_ref[...]) * (1.0 / GLA_GATE_NORM)


def _gla_gate(x, g, w1p, w2p, b_g, *, tm=512):
    T, D = x.shape
    KD = w2p.shape[1]
    return pl.pallas_call(
        _gla_gate_body,
        grid=(T // tm,),
        in_specs=[
            pl.BlockSpec((tm, D), lambda i: (i, 0)),
            pl.BlockSpec((1, D), lambda i: (0, 0)),
            pl.BlockSpec((D, LANES), lambda i: (0, 0)),
            pl.BlockSpec((LANES, KD), lambda i: (0, 0)),
            pl.BlockSpec((1, KD), lambda i: (0, 0)),
        ],
        out_specs=pl.BlockSpec((tm, KD), lambda i: (i, 0)),
        out_shape=jax.ShapeDtypeStruct((T, KD), F32),
        compiler_params=_params("parallel"),
        name="gla_log_decay",
    )(x, g, w1p, w2p, b_g)


def _gla_chunk_body(q_ref, k_ref, v_ref, g_ref, gn_ref, o_ref, st_ref, *, tb, scale):
    C = GLA_CHUNK

    @pl.when(pl.program_id(2) == 0)
    def _():
        st_ref[...] = jnp.zeros_like(st_ref)

    r = lax.broadcasted_iota(jnp.int32, (tb, tb), 0)
    c = lax.broadcasted_iota(jnp.int32, (tb, tb), 1)
    shift = C.bit_length() - 1
    same = jnp.right_shift(r, shift) == jnp.right_shift(c, shift)
    tril_blk = jnp.where(jnp.logical_and(same, c <= r), 1.0, 0.0).astype(BF16)
    b_all = _dot_01(tril_blk, g_ref[...])

    rc = lax.broadcasted_iota(jnp.int32, (C, C), 0)
    cc = lax.broadcasted_iota(jnp.int32, (C, C), 1)
    causal = cc <= rc

    for n in range(tb // C):
        r0 = n * C
        qc = q_ref[r0:r0 + C, :].astype(F32) * scale
        kc = k_ref[r0:r0 + C, :].astype(F32)
        vc = v_ref[r0:r0 + C, :]
        b = b_all[r0:r0 + C, :]
        b_last = b[C - 1:C, :]
        b_mid = b[C // 2:C // 2 + 1, :]
        st = st_ref[...]
        o_inter = _dot_nt((qc * jnp.exp(b)).astype(BF16), st.astype(BF16))
        qs = (qc * jnp.exp(b - b_mid)).astype(BF16)
        ks = (kc * jnp.exp(b_mid - b)).astype(BF16)
        scores = jnp.where(causal, _dot_nt(qs, ks), 0.0)
        o = o_inter + _dot(scores.astype(BF16), vc)
        k_dec = (kc * jnp.exp(b_last - b)).astype(BF16)
        st_ref[...] = st * jnp.exp(b_last) + _dot_tn(vc, k_dec)
        o_ref[r0:r0 + C, :] = _rms(o, gn_ref[...]).astype(o_ref.dtype)


def _gla_chunks(u, gdec, gn_g, *, B, S, E, tb=256):
    T = u.shape[0]
    Hh = GLA_HEADS
    KD = gdec.shape[1]
    dk, dv = KD // Hh, E // Hh
    nS = S // tb
    body = functools.partial(_gla_chunk_body, tb=tb, scale=dk ** -0.5)
    return pl.pallas_call(
        body,
        grid=(B, Hh, nS),
        in_specs=[
            pl.BlockSpec((tb, dk), lambda b, h, i: (b * nS + i, h)),
            pl.BlockSpec((tb, dk), lambda b, h, i: (b * nS + i, Hh + h)),
            pl.BlockSpec((tb, dv), lambda b, h, i: (b * nS + i, (2 * KD) // dv + h)),
            pl.BlockSpec((tb, dk), lambda b, h, i: (b * nS + i, h)),
            pl.BlockSpec((1, dv), lambda b, h, i: (0, 0)),
        ],
        out_specs=pl.BlockSpec((tb, dv), lambda b, h, i: (b * nS + i, h)),
        out_shape=jax.ShapeDtypeStruct((T, E), BF16),
        scratch_shapes=[pltpu.VMEM((dv, dk), F32)],
        compiler_params=_params("parallel", "parallel", "arbitrary"),
        name="gla_chunks",
    )(u, u, u, gdec, gn_g)


def _gla_layer(x, g, w_in, w_g1, w_g2, b_g, gn_g, w_out, *, B, S):
    E = w_out.shape[0]
    KD = w_g2.shape[1]
    rank = w_g1.shape[1]
    u = _rms_matmul(x, g, w_in.astype(BF16))
    w1p = jnp.pad(w_g1, ((0, 0), (0, LANES - rank))).astype(BF16)
    w2p = jnp.pad(w_g2, ((0, LANES - rank), (0, 0))).astype(BF16)
    gdec = _gla_gate(x, g, w1p, w2p, b_g.reshape(1, KD))
    o = _gla_chunks(u, gdec, gn_g.reshape(1, -1), B=B, S=S, E=E)
    return _gated_out(o, u, (2 * KD + E) // E, w_out.astype(BF16), x)


def _lru_body(xb_ref, xh_ref, gate_ref, x_ref, wc_ref, bc_ref, wax_ref, ba_ref, bx_ref, lam_ref,
              wout_ref, fg_ref, y_ref, buf_ref, a_ref, b_ref, h_ref, carry_ref, *, tm, E):
    i = pl.program_id(1)

    @pl.when(i == 0)
    def _():
        carry_ref[...] = jnp.zeros_like(carry_ref)

    buf_ref[0:LRU_HALO, :] = jnp.where(i > 0, xh_ref[...].astype(F32), 0.0)
    buf_ref[LRU_HALO:, :] = xb_ref[...].astype(F32)
    off = LRU_HALO - (LRU_CONV_K - 1)
    xc = jnp.broadcast_to(bc_ref[...], (tm, E))
    for k in range(LRU_CONV_K):
        xc = xc + wc_ref[k:k + 1, :] * buf_ref[off + k:off + k + tm, :]

    lam = lam_ref[...]
    sp = jnp.maximum(-lam, 0.0) + jnp.log(1.0 + jnp.exp(-jnp.abs(lam)))
    W = LRU_BLOCK_W
    for n in range(E // W):
        sl = slice(n * W, (n + 1) * W)
        xn = xc[:, sl]
        pre = _dot(xn.astype(BF16), wax_ref[n])
        r = _sigmoid(pre[:, :W] + ba_ref[:, sl])
        ig = _sigmoid(pre[:, W:] + bx_ref[:, sl])
        log_a = (-LRU_C * r) * sp[:, sl]
        a = jnp.exp(log_a)
        beta = jnp.sqrt(1.0 - a * a)
        a_ref[:, sl] = a
        b_ref[:, sl] = beta * (ig * xn)

    def step(t, h):
        h = a_ref[pl.ds(t, 1), :] * h + b_ref[pl.ds(t, 1), :]
        h_ref[pl.ds(t, 1), :] = h
        return h

    carry_ref[...] = lax.fori_loop(0, tm, step, carry_ref[...], unroll=8)

    og = (h_ref[...] * _silu(gate_ref[...].astype(F32))).astype(BF16)
    y = x_ref[...] + _dot(og, wout_ref[...])
    y_ref[...] = _rms(y, fg_ref[...])


def _lru_layer(x, g, w_in, w_conv, b_conv, w_a, b_a, w_x, b_x, lam, w_out, final_g, *, B, S, tm=256):
    T, D = x.shape
    E = w_out.shape[0]
    u = _rms_matmul(x, g, w_in.astype(BF16))
    wax = jnp.concatenate([w_a, w_x], axis=-1).astype(BF16)
    nS = S // tm
    hpb = tm // LRU_HALO

    def row(n):
        return pl.BlockSpec((1, n), lambda b, i: (0, 0))

    body = functools.partial(_lru_body, tm=tm, E=E)
    return pl.pallas_call(
        body,
        grid=(B, nS),
        in_specs=[
            pl.BlockSpec((tm, E), lambda b, i: (b * nS + i, 0)),
            pl.BlockSpec((LRU_HALO, E), lambda b, i: (jnp.maximum((b * nS + i) * hpb - 1, 0), 0)),
            pl.BlockSpec((tm, E), lambda b, i: (b * nS + i, 1)),
            pl.BlockSpec((tm, D), lambda b, i: (b * nS + i, 0)),
            pl.BlockSpec((LRU_CONV_K, E), lambda b, i: (0, 0)),
            row(E),
            pl.BlockSpec(wax.shape, lambda b, i: (0, 0, 0)),
            row(E), row(E), row(E),
            pl.BlockSpec((E, D), lambda b, i: (0, 0)),
            row(D),
        ],
        out_specs=pl.BlockSpec((tm, D), lambda b, i: (b * nS + i, 0)),
        out_shape=jax.ShapeDtypeStruct((T, D), F32),
        scratch_shapes=[pltpu.VMEM((tm + LRU_HALO, E), F32), pltpu.VMEM((tm, E), F32),
                        pltpu.VMEM((tm, E), F32), pltpu.VMEM((tm, E), F32), pltpu.VMEM((1, E), F32)],
        compiler_params=_params("parallel", "arbitrary"),
        name="rglru_mixer",
    )(u, u, u, x, w_conv, b_conv.reshape(1, E), wax, b_a.reshape(1, E), b_x.reshape(1, E),
      lam.reshape(1, E), w_out.astype(BF16), final_g.reshape(1, D))


def kernel(x, norm_g, final_g, conv_w_in, conv_w_dw, conv_b_dw, conv_ln_g, conv_ln_b, conv_w_out, fox_w_in, fox_b_f, fox_w_out, gla_w_in, gla_w_g1, gla_w_g2, gla_b_g, gla_gn_g, gla_w_out, lru_w_in, lru_w_conv, lru_b_conv, lru_w_a, lru_b_a, lru_w_x, lru_b_x, lru_lam, lru_w_out):
    B, S, D = x.shape
    assert norm_g.shape[0] == 4 and conv_w_in.shape[0] == 1, "one layer per mixer"
    E = conv_w_out.shape[1]
    h = x.reshape(B * S, D)
    ng = norm_g.reshape(4, 1, D)
    h = _conv_layer(h, ng[0], conv_w_in[0].astype(BF16), conv_w_dw[0], conv_b_dw[0].reshape(1, E),
                    conv_ln_g[0].reshape(1, E), conv_ln_b[0].reshape(1, E), conv_w_out[0].astype(BF16), B=B, S=S)
    h = _fox_layer(h, ng[1], fox_w_in[0], fox_b_f[0], fox_w_out[0], B=B, S=S)
    h = _gla_layer(h, ng[2], gla_w_in[0], gla_w_g1[0], gla_w_g2[0], gla_b_g[0], gla_gn_g[0], gla_w_out[0], B=B, S=S)
    h = _lru_layer(h, ng[3], lru_w_in[0], lru_w_conv[0], lru_b_conv[0], lru_w_a[0], lru_b_a[0], lru_w_x[0],
                   lru_b_x[0], lru_lam[0], lru_w_out[0], final_g, B=B, S=S)
    return h.reshape(B, S, D)
```

```python
import functools

import jax
import jax.numpy as jnp
from jax import lax
from jax.experimental import pallas as pl
from jax.experimental.pallas import tpu as pltpu

F32 = jnp.float32
BF16 = jnp.bfloat16
EPS = 1e-6

V7X_VMEM_BYTES = 64 * 1024 * 1024
VMEM_LIMIT_BYTES = V7X_VMEM_BYTES - 16 * 1024 * 1024
LANES = 128
SUBLANES = 8

CONV_K = 31
CONV_HALO = 32
FOX_HEAD_DIM = 128
GLA_HEADS = 4
GLA_CHUNK = 64
GLA_GATE_NORM = 16.0
LRU_CONV_K = 4
LRU_HALO = 16
LRU_BLOCK_W = 128
LRU_C = 8.0


def _params(*sem):
    return pltpu.CompilerParams(dimension_semantics=sem, vmem_limit_bytes=VMEM_LIMIT_BYTES)


def _sigmoid(x):
    return 1.0 / (1.0 + jnp.exp(-x))


def _silu(x):
    return x * _sigmoid(x)


def _log_sigmoid(x):
    return jnp.minimum(x, 0.0) - jnp.log(1.0 + jnp.exp(-jnp.abs(x)))


def _rms(x, g):
    return x * lax.rsqrt(jnp.mean(x * x, axis=-1, keepdims=True) + EPS) * g


def _dot(a, b):
    return jnp.dot(a, b, preferred_element_type=F32)


def _dot_nt(a, b):
    return lax.dot_general(a, b, (((1,), (1,)), ((), ())), preferred_element_type=F32)


def _dot_tn(a, b):
    return lax.dot_general(a, b, (((0,), (0,)), ((), ())), preferred_element_type=F32)


def _dot_01(m01, x):
    x1 = x.astype(BF16)
    r1 = x - x1.astype(F32)
    x2 = r1.astype(BF16)
    x3 = (r1 - x2.astype(F32)).astype(BF16)
    return _dot(m01, x1) + _dot(m01, x2) + _dot(m01, x3)


def _rms_matmul_body(x_ref, g_ref, w_ref, o_ref, hn_ref):
    @pl.when(pl.program_id(1) == 0)
    def _():
        hn_ref[...] = _rms(x_ref[...], g_ref[...]).astype(BF16)

    o_ref[...] = _dot(hn_ref[...], w_ref[...]).astype(o_ref.dtype)


def _rms_matmul(x, g, w, *, tm=1024, tn=1024):
    T, D = x.shape
    N = w.shape[1]
    assert T % tm == 0 and N % tn == 0
    return pl.pallas_call(
        _rms_matmul_body,
        grid=(T // tm, N // tn),
        in_specs=[
            pl.BlockSpec((tm, D), lambda i, j: (i, 0)),
            pl.BlockSpec((1, D), lambda i, j: (0, 0)),
            pl.BlockSpec((D, tn), lambda i, j: (0, j)),
        ],
        out_specs=pl.BlockSpec((tm, tn), lambda i, j: (i, j)),
        out_shape=jax.ShapeDtypeStruct((T, N), BF16),
        scratch_shapes=[pltpu.VMEM((tm, D), BF16)],
        compiler_params=_params("parallel", "arbitrary"),
        name="rms_in_proj",
    )(x, g, w)


def _gated_out_body(o_ref, gate_ref, w_ref, x_ref, y_ref):
    og = (o_ref[...].astype(F32) * _silu(gate_ref[...].astype(F32))).astype(BF16)
    y_ref[...] = x_ref[...] + _dot(og, w_ref[...])


def _gated_out(o, u, gate_blk, w_out, x, *, tm=512):
    T, E = o.shape
    D = x.shape[1]
    assert T % tm == 0
    return pl.pallas_call(
        _gated_out_body,
        grid=(T // tm,),
        in_specs=[
            pl.BlockSpec((tm, E), lambda i: (i, 0)),
            pl.BlockSpec((tm, E), lambda i: (i, gate_blk)),
            pl.BlockSpec((E, D), lambda i: (0, 0)),
            pl.BlockSpec((tm, D), lambda i: (i, 0)),
        ],
        out_specs=pl.BlockSpec((tm, D), lambda i: (i, 0)),
        out_shape=jax.ShapeDtypeStruct((T, D), F32),
        compiler_params=_params("parallel"),
        name="gated_out_proj",
    )(o, u, w_out, x)


def _conv_body(a_ref, b_ref, gate_ref, ah_ref, bh_ref, x_ref, wdw_ref, bdw_ref, lng_ref, lnb_ref,
               wout_ref, y_ref, sh_ref, acc_ref, *, tm, E, rows, cols):
    i = pl.program_id(1)
    off = CONV_HALO - (CONV_K - 1)
    n_sh = tm + CONV_HALO - SUBLANES
    for c0 in range(0, E, cols):
        cs = slice(c0, c0 + cols)
        vh = ah_ref[:, cs].astype(F32) * _sigmoid(bh_ref[:, cs].astype(F32))
        sh_ref[0, 0:CONV_HALO, :] = jnp.where(i > 0, vh, 0.0)
        sh_ref[0, CONV_HALO:, :] = a_ref[:, cs].astype(F32) * _sigmoid(b_ref[:, cs].astype(F32))
        for s in range(1, SUBLANES):
            sh_ref[s, 0:n_sh, :] = sh_ref[0, s:s + n_sh, :]
        for r0 in range(0, tm, rows):
            acc = jnp.broadcast_to(bdw_ref[:, cs], (rows, cols))
            for k in range(CONV_K):
                s, a8 = (off + k) % SUBLANES, (off + k) // SUBLANES * SUBLANES
                acc = acc + wdw_ref[k:k + 1, cs] * sh_ref[s, r0 + a8:r0 + a8 + rows, :]
            acc_ref[r0:r0 + rows, cs] = acc

    c = acc_ref[...]
    mu = jnp.mean(c, axis=-1, keepdims=True)
    d = c - mu
    var = jnp.mean(d * d, axis=-1, keepdims=True)
    ln = d * lax.rsqrt(var + EPS) * lng_ref[...] + lnb_ref[...]
    og = (_silu(ln) * _silu(gate_ref[...].astype(F32))).astype(BF16)
    y_ref[...] = x_ref[...] + _dot(og, wout_ref[...])


def _conv_layer(x, g, w_in, w_dw, b_dw, ln_g, ln_b, w_out, *, B, S, tm=256):
    T, D = x.shape
    E = w_out.shape[0]
    u = _rms_matmul(x, g, w_in)
    nS = S // tm
    hpb = tm // CONV_HALO

    def tile(col):
        return pl.BlockSpec((tm, E), lambda b, i: (b * nS + i, col))

    def halo(col):
        return pl.BlockSpec((CONV_HALO, E), lambda b, i: (jnp.maximum((b * nS + i) * hpb - 1, 0), col))

    def row(n):
        return pl.BlockSpec((1, n), lambda b, i: (0, 0))

    cols = 512
    body = functools.partial(_conv_body, tm=tm, E=E, rows=32, cols=cols)
    return pl.pallas_call(
        body,
        grid=(B, nS),
        in_specs=[tile(0), tile(1), tile(2), halo(0), halo(1),
                  pl.BlockSpec((tm, D), lambda b, i: (b * nS + i, 0)),
                  pl.BlockSpec((CONV_K, E), lambda b, i: (0, 0)),
                  row(E), row(E), row(E),
                  pl.BlockSpec((E, D), lambda b, i: (0, 0))],
        out_specs=pl.BlockSpec((tm, D), lambda b, i: (b * nS + i, 0)),
        out_shape=jax.ShapeDtypeStruct((T, D), F32),
        scratch_shapes=[pltpu.VMEM((SUBLANES, tm + CONV_HALO, cols), F32), pltpu.VMEM((tm, E), F32)],
        compiler_params=_params("parallel", "parallel"),
        name="conv_mixer",
    )(u, u, u, u, u, x, w_dw, b_dw, ln_g, ln_b, w_out)


def _fox_cum_body(x_ref, g_ref, wf_ref, bf_ref, o_ref, carry_ref, *, tc, H):
    @pl.when(pl.program_id(1) == 0)
    def _():
        carry_ref[...] = jnp.zeros_like(carry_ref)

    hn = _rms(x_ref[...], g_ref[...]).astype(BF16)
    log_f = _log_sigmoid(_dot(hn, wf_ref[...]) + bf_ref[...])
    r = lax.broadcasted_iota(jnp.int32, (tc, tc), 0)
    c = lax.broadcasted_iota(jnp.int32, (tc, tc), 1)
    tril = jnp.where(c <= r, 1.0, 0.0).astype(BF16)
    cum = _dot_01(tril, log_f) + carry_ref[...]
    carry_ref[...] = cum[tc - 1:tc, :]
    o_ref[0] = cum.T[0:H, :]


def _fox_cum(x, g, wf, bfp, *, B, S, H, tc=512):
    T, D = x.shape
    nS = S // tc
    body = functools.partial(_fox_cum_body, tc=tc, H=H)
    return pl.pallas_call(
        body,
        grid=(B, nS),
        in_specs=[
            pl.BlockSpec((tc, D), lambda b, i: (b * nS + i, 0)),
            pl.BlockSpec((1, D), lambda b, i: (0, 0)),
            pl.BlockSpec((D, LANES), lambda b, i: (0, 0)),
            pl.BlockSpec((1, LANES), lambda b, i: (0, 0)),
        ],
        out_specs=pl.BlockSpec((1, H, tc), lambda b, i: (b, 0, i)),
        out_shape=jax.ShapeDtypeStruct((B, H, S), F32),
        scratch_shapes=[pltpu.VMEM((1, LANES), F32)],
        compiler_params=_params("parallel", "arbitrary"),
        name="fox_forget_cumsum",
    )(x, g, wf, bfp)


def _fox_attn_body(q_ref, k_ref, v_ref, c_ref, o_ref, vt_ref, cc_ref, s0_ref, s1_ref, *, S, tq, tk, ch, scale):
    dh = q_ref.shape[1]
    log2e = 1.4426950408889634
    assert tq == 2 * tk, "a query tile spans exactly two kv blocks (pairwise pipelined loop)"

    for n0 in range(S // ch):
        rows = slice(n0 * ch, (n0 + 1) * ch)
        vt = v_ref[rows, :].astype(F32).T.astype(BF16)
        for s0 in range(ch // tk):
            vt_ref[n0 * (ch // tk) + s0] = vt[:, s0 * tk:(s0 + 1) * tk]
        crow = c_ref[0, 0, n0:n0 + 1, :] * log2e
        cc_ref[rows, :] = jnp.broadcast_to(crow, (LANES, ch)).T

    def q_tile(i, _):
        q0 = pl.multiple_of(i * tq, tq)
        qw = (q_ref[pl.ds(q0, tq), :].astype(F32) * (scale * log2e)).astype(BF16)
        c0 = cc_ref[pl.ds(q0, 1), :]

        def logits(j, s_ref):
            kv0 = pl.multiple_of(j * tk, tk)
            bias = c0 - cc_ref[pl.ds(kv0, tk), :]
            st = _dot_nt(k_ref[pl.ds(kv0, tk), :], qw)
            s_ref[...] = st + jnp.concatenate([bias] * (tq // LANES), axis=1)

        def consume(j, s_ref, carry, diag):
            m, l, acc = carry
            st = s_ref[...]
            if diag is not None:
                r = lax.broadcasted_iota(jnp.int32, (tk, tq), 0) + diag * tk
                c = lax.broadcasted_iota(jnp.int32, (tk, tq), 1)
                st = jnp.where(r <= c, st, -jnp.inf)
            m_new = jnp.maximum(m, jnp.max(st, axis=0, keepdims=True))
            alpha = jnp.exp2(m - m_new)
            p = jnp.exp2(st - m_new)
            l = alpha * l + jnp.sum(p, axis=0, keepdims=True)
            acc = alpha * acc + _dot(vt_ref[j], p.astype(BF16))
            return m_new, l, acc

        def pair(jj, carry):
            j = 2 * jj
            logits(j + 1, s1_ref)
            carry = consume(j, s0_ref, carry, None)
            logits(j + 2, s0_ref)
            return consume(j + 1, s1_ref, carry, None)

        carry = (jnp.full((1, tq), -jnp.inf, F32), jnp.zeros((1, tq), F32), jnp.zeros((dh, tq), F32))
        logits(0, s0_ref)
        carry = lax.fori_loop(0, i, pair, carry)
        logits(2 * i + 1, s1_ref)
        carry = consume(2 * i, s0_ref, carry, 0)
        _, l, acc = consume(2 * i + 1, s1_ref, carry, 1)
        o_ref[pl.ds(q0, tq), :] = (acc / l).T.astype(o_ref.dtype)
        return 0

    lax.fori_loop(0, S // tq, q_tile, 0)


def _fox_attention(u, cum, *, B, S, H, E, tq=1024, tk=512, ch=512):
    T = u.shape[0]
    dh = FOX_HEAD_DIM
    cum4 = cum.reshape(B, H, S // ch, ch)
    body = functools.partial(_fox_attn_body, S=S, tq=tq, tk=tk, ch=ch, scale=dh ** -0.5)
    return pl.pallas_call(
        body,
        grid=(B, H),
        in_specs=[
            pl.BlockSpec((S, dh), lambda b, h: (b, h)),
            pl.BlockSpec((S, dh), lambda b, h: (b, H + h)),
            pl.BlockSpec((S, dh), lambda b, h: (b, 2 * H + h)),
            pl.BlockSpec((1, 1, S // ch, ch), lambda b, h: (b, h, 0, 0)),
        ],
        out_specs=pl.BlockSpec((S, dh), lambda b, h: (b, h)),
        out_shape=jax.ShapeDtypeStruct((T, E), BF16),
        scratch_shapes=[pltpu.VMEM((S // tk, dh, tk), BF16), pltpu.VMEM((S, LANES), F32),
                        pltpu.VMEM((tk, tq), F32), pltpu.VMEM((tk, tq), F32)],
        compiler_params=_params("parallel", "parallel"),
        name="fox_attention",
    )(u, u, u, cum4)


def _fox_layer(x, g, w_in, b_f, w_out, *, B, S):
    E = w_out.shape[0]
    H = E // FOX_HEAD_DIM
    w_main = w_in[:, :3 * E]
    w_f = w_in[:, 3 * E:3 * E + H]
    w_gate = w_in[:, 3 * E + H:]
    w_cat = jnp.concatenate([w_main, w_gate], axis=1).astype(BF16)
    wf_pad = jnp.pad(w_f, ((0, 0), (0, LANES - H))).astype(BF16)
    bf_pad = jnp.pad(b_f, (0, LANES - H)).reshape(1, LANES)
    u = _rms_matmul(x, g, w_cat)
    cum = _fox_cum(x, g, wf_pad, bf_pad, B=B, S=S, H=H)
    o = _fox_attention(u, cum, B=B, S=S, H=H, E=E)
    return _gated_out(o, u, 3, w_out.astype(BF16), x)


def _gla_gate_body(x_ref, g_ref, w1_ref, w2_ref, b_ref, o_ref):
    hn = _rms(x_ref[...], g_ref[...]).astype(BF16)
    low = _dot(hn, w1_ref[...]).astype(BF16)
    o_ref[...] = _log_sigmoid(_dot(low, w2_ref[...]) + b_ref[...]) * (1.0 / GLA_GATE_NORM)


def _gla_gate(x, g, w1p, w2p, b_g, *, tm=512):
    T, D = x.shape
    KD = w2p.shape[1]
    return pl.pallas_call(
        _gla_gate_body,
        grid=(T // tm,),
        in_specs=[
            pl.BlockSpec((tm, D), lambda i: (i, 0)),
            pl.BlockSpec((1, D), lambda i: (0, 0)),
            pl.BlockSpec((D, LANES), lambda i: (0, 0)),
            pl.BlockSpec((LANES, KD), lambda i: (0, 0)),
            pl.BlockSpec((1, KD), lambda i: (0, 0)),
        ],
        out_specs=pl.BlockSpec((tm, KD), lambda i: (i, 0)),
        out_shape=jax.ShapeDtypeStruct((T, KD), F32),
        compiler_params=_params("parallel"),
        name="gla_log_decay",
    )(x, g, w1p, w2p, b_g)


def _gla_chunk_body(q_ref, k_ref, v_ref, g_ref, gn_ref, o_ref, st_ref, *, tb, scale):
    C = GLA_CHUNK

    @pl.when(pl.program_id(2) == 0)
    def _():
        st_ref[...] = jnp.zeros_like(st_ref)

    r = lax.broadcasted_iota(jnp.int32, (tb, tb), 0)
    c = lax.broadcasted_iota(jnp.int32, (tb, tb), 1)
    shift = C.bit_length() - 1
    same = jnp.right_shift(r, shift) == jnp.right_shift(c, shift)
    tril_blk = jnp.where(jnp.logical_and(same, c <= r), 1.0, 0.0).astype(BF16)
    b_all = _dot_01(tril_blk, g_ref[...])

    rc = lax.broadcasted_iota(jnp.int32, (C, C), 0)
    cc = lax.broadcasted_iota(jnp.int32, (C, C), 1)
    causal = cc <= rc

    for n in range(tb // C):
        r0 = n * C
        qc = q_ref[r0:r0 + C, :].astype(F32) * scale
        kc = k_ref[r0:r0 + C, :].astype(F32)
        vc = v_ref[r0:r0 + C, :]
        b = b_all[r0:r0 + C, :]
        b_last = b[C - 1:C, :]
        b_mid = b[C // 2:C // 2 + 1, :]
        st = st_ref[...]
        o_inter = _dot_nt((qc * jnp.exp(b)).astype(BF16), st.astype(BF16))
        qs = (qc * jnp.exp(b - b_mid)).astype(BF16)
        ks = (kc * jnp.exp(b_mid - b)).astype(BF16)
        scores = jnp.where(causal, _dot_nt(qs, ks), 0.0)
        o = o_inter + _dot(scores.astype(BF16), vc)
        k_dec = (kc * jnp.exp(b_last - b)).astype(BF16)
        st_ref[...] = st * jnp.exp(b_last) + _dot_tn(vc, k_dec)
        o_ref[r0:r0 + C, :] = _rms(o, gn_ref[...]).astype(o_ref.dtype)


def _gla_chunks(u, gdec, gn_g, *, B, S, E, tb=256):
    T = u.shape[0]
    Hh = GLA_HEADS
    KD = gdec.shape[1]
    dk, dv = KD // Hh, E // Hh
    nS = S // tb
    body = functools.partial(_gla_chunk_body, tb=tb, scale=dk ** -0.5)
    return pl.pallas_call(
        body,
        grid=(B, Hh, nS),
        in_specs=[
            pl.BlockSpec((tb, dk), lambda b, h, i: (b * nS + i, h)),
            pl.BlockSpec((tb, dk), lambda b, h, i: (b * nS + i, Hh + h)),
            pl.BlockSpec((tb, dv), lambda b, h, i: (b * nS + i, (2 * KD) // dv + h)),
            pl.BlockSpec((tb, dk), lambda b, h, i: (b * nS + i, h)),
            pl.BlockSpec((1, dv), lambda b, h, i: (0, 0)),
        ],
        out_specs=pl.BlockSpec((tb, dv), lambda b, h, i: (b * nS + i, h)),
        out_shape=jax.ShapeDtypeStruct((T, E), BF16),
        scratch_shapes=[pltpu.VMEM((dv, dk), F32)],
        compiler_params=_params("parallel", "parallel", "arbitrary"),
        name="gla_chunks",
    )(u, u, u, gdec, gn_g)


def _gla_layer(x, g, w_in, w_g1, w_g2, b_g, gn_g, w_out, *, B, S):
    E = w_out.shape[0]
    KD = w_g2.shape[1]
    rank = w_g1.shape[1]
    u = _rms_matmul(x, g, w_in.astype(BF16))
    w1p = jnp.pad(w_g1, ((0, 0), (0, LANES - rank))).astype(BF16)
    w2p = jnp.pad(w_g2, ((0, LANES - rank), (0, 0))).astype(BF16)
    gdec = _gla_gate(x, g, w1p, w2p, b_g.reshape(1, KD))
    o = _gla_chunks(u, gdec, gn_g.reshape(1, -1), B=B, S=S, E=E)
    return _gated_out(o, u, (2 * KD + E) // E, w_out.astype(BF16), x)


def _lru_body(xb_ref, xh_ref, gate_ref, x_ref, wc_ref, bc_ref, wax_ref, ba_ref, bx_ref, lam_ref,
              wout_ref, fg_ref, y_ref, buf_ref, a_ref, b_ref, h_ref, carry_ref, *, tm, E):
    i = pl.program_id(1)

    @pl.when(i == 0)
    def _():
        carry_ref[...] = jnp.zeros_like(carry_ref)

    buf_ref[0:LRU_HALO, :] = jnp.where(i > 0, xh_ref[...].astype(F32), 0.0)
    buf_ref[LRU_HALO:, :] = xb_ref[...].astype(F32)
    off = LRU_HALO - (LRU_CONV_K - 1)
    xc = jnp.broadcast_to(bc_ref[...], (tm, E))
    for k in range(LRU_CONV_K):
        xc = xc + wc_ref[k:k + 1, :] * buf_ref[off + k:off + k + tm, :]

    lam = lam_ref[...]
    sp = jnp.maximum(-lam, 0.0) + jnp.log(1.0 + jnp.exp(-jnp.abs(lam)))
    W = LRU_BLOCK_W
    for n in range(E // W):
        sl = slice(n * W, (n + 1) * W)
        xn = xc[:, sl]
        pre = _dot(xn.astype(BF16), wax_ref[n])
        r = _sigmoid(pre[:, :W] + ba_ref[:, sl])
        ig = _sigmoid(pre[:, W:] + bx_ref[:, sl])
        log_a = (-LRU_C * r) * sp[:, sl]
        a = jnp.exp(log_a)
        beta = jnp.sqrt(1.0 - a * a)
        a_ref[:, sl] = a
        b_ref[:, sl] = beta * (ig * xn)

    def step(t, h):
        h = a_ref[pl.ds(t, 1), :] * h + b_ref[pl.ds(t, 1), :]
        h_ref[pl.ds(t, 1), :] = h
        return h

    carry_ref[...] = lax.fori_loop(0, tm, step, carry_ref[...], unroll=8)

    og = (h_ref[...] * _silu(gate_ref[...].astype(F32))).astype(BF16)
    y = x_ref[...] + _dot(og, wout_ref[...])
    y_ref[...] = _rms(y, fg_ref[...])


def _lru_layer(x, g, w_in, w_conv, b_conv, w_a, b_a, w_x, b_x, lam, w_out, final_g, *, B, S, tm=256):
    T, D = x.shape
    E = w_out.shape[0]
    u = _rms_matmul(x, g, w_in.astype(BF16))
    wax = jnp.concatenate([w_a, w_x], axis=-1).astype(BF16)
    nS = S // tm
    hpb = tm // LRU_HALO

    def row(n):
        return pl.BlockSpec((1, n), lambda b, i: (0, 0))

    body = functools.partial(_lru_body, tm=tm, E=E)
    return pl.pallas_call(
        body,
        grid=(B, nS),
        in_specs=[
            pl.BlockSpec((tm, E), lambda b, i: (b * nS + i, 0)),
            pl.BlockSpec((LRU_HALO, E), lambda b, i: (jnp.maximum((b * nS + i) * hpb - 1, 0), 0)),
            pl.BlockSpec((tm, E), lambda b, i: (b * nS + i, 1)),
            pl.BlockSpec((tm, D), lambda b, i: (b * nS + i, 0)),
            pl.BlockSpec((LRU_CONV_K, E), lambda b, i: (0, 0)),
            row(E),
            pl.BlockSpec(wax.shape, lambda b, i: (0, 0, 0)),
            row(E), row(E), row(E),
            pl.BlockSpec((E, D), lambda b, i: (0, 0)),
            row(D),
        ],
        out_specs=pl.BlockSpec((tm, D), lambda b, i: (b * nS + i, 0)),
        out_shape=jax.ShapeDtypeStruct((T, D), F32),
        scratch_shapes=[pltpu.VMEM((tm + LRU_HALO, E), F32), pltpu.VMEM((tm, E), F32),
                        pltpu.VMEM((tm, E), F32), pltpu.VMEM((tm, E), F32), pltpu.VMEM((1, E), F32)],
        compiler_params=_params("parallel", "arbitrary"),
        name="rglru_mixer",
    )(u, u, u, x, w_conv, b_conv.reshape(1, E), wax, b_a.reshape(1, E), b_x.reshape(1, E),
      lam.reshape(1, E), w_out.astype(BF16), final_g.reshape(1, D))


def kernel(x, norm_g, final_g, conv_w_in, conv_w_dw, conv_b_dw, conv_ln_g, conv_ln_b, conv_w_out, fox_w_in, fox_b_f, fox_w_out, gla_w_in, gla_w_g1, gla_w_g2, gla_b_g, gla_gn_g, gla_w_out, lru_w_in, lru_w_conv, lru_b_conv, lru_w_a, lru_b_a, lru_w_x, lru_b_x, lru_lam, lru_w_out):
    B, S, D = x.shape
    assert norm_g.shape[0] == 4 and conv_w_in.shape[0] == 1, "one layer per mixer"
    E = conv_w_out.shape[1]
    h = x.reshape(B * S, D)
    ng = norm_g.reshape(4, 1, D)
    h = _conv_layer(h, ng[0], conv_w_in[0].astype(BF16), conv_w_dw[0], conv_b_dw[0].reshape(1, E),
                    conv_ln_g[0].reshape(1, E), conv_ln_b[0].reshape(1, E), conv_w_out[0].astype(BF16), B=B, S=S)
    h = _fox_layer(h, ng[1], fox_w_in[0], fox_b_f[0], fox_w_out[0], B=B, S=S)
    h = _gla_layer(h, ng[2], gla_w_in[0], gla_w_g1[0], gla_w_g2[0], gla_b_g[0], gla_gn_g[0], gla_w_out[0], B=B, S=S)
    h = _lru_layer(h, ng[3], lru_w_in[0], lru_w_conv[0], lru_b_conv[0], lru_w_a[0], lru_b_a[0], lru_w_x[0],
                   lru_b_x[0], lru_lam[0], lru_w_out[0], final_g, B=B, S=S)
    return h.reshape(B, S, D)
```

```python
import functools

import jax
import jax.numpy as jnp
from jax import lax
from jax.experimental import pallas as pl
from jax.experimental.pallas import tpu as pltpu

F32 = jnp.float32
BF16 = jnp.bfloat16
EPS = 1e-6

V7X_VMEM_BYTES = 64 * 1024 * 1024
VMEM_LIMIT_BYTES = V7X_VMEM_BYTES - 16 * 1024 * 1024
LANES = 128
SUBLANES = 8

CONV_K = 31
CONV_HALO = 32
FOX_HEAD_DIM = 128
ONES_ROWS = 16
GLA_HEADS = 4
GLA_CHUNK = 64
GLA_GATE_NORM = 16.0
LRU_CONV_K = 4
LRU_HALO = 16
LRU_BLOCK_W = 128
LRU_C = 8.0


def _params(*sem):
    return pltpu.CompilerParams(dimension_semantics=sem, vmem_limit_bytes=VMEM_LIMIT_BYTES)


def _sigmoid(x):
    return 1.0 / (1.0 + jnp.exp(-x))


def _silu(x):
    return x * _sigmoid(x)


def _log_sigmoid(x):
    return jnp.minimum(x, 0.0) - jnp.log(1.0 + jnp.exp(-jnp.abs(x)))


def _rms(x, g):
    return x * lax.rsqrt(jnp.mean(x * x, axis=-1, keepdims=True) + EPS) * g


def _dot(a, b):
    return jnp.dot(a, b, preferred_element_type=F32)


def _dot_nt(a, b):
    return lax.dot_general(a, b, (((1,), (1,)), ((), ())), preferred_element_type=F32)


def _dot_tn(a, b):
    return lax.dot_general(a, b, (((0,), (0,)), ((), ())), preferred_element_type=F32)


def _dot_01(m01, x):
    x1 = x.astype(BF16)
    r1 = x - x1.astype(F32)
    x2 = r1.astype(BF16)
    x3 = (r1 - x2.astype(F32)).astype(BF16)
    return _dot(m01, x1) + _dot(m01, x2) + _dot(m01, x3)


def _rms_matmul_body(x_ref, g_ref, w_ref, o_ref, hn_ref):
    @pl.when(pl.program_id(1) == 0)
    def _():
        hn_ref[...] = _rms(x_ref[...], g_ref[...]).astype(BF16)

    o_ref[...] = _dot(hn_ref[...], w_ref[...]).astype(o_ref.dtype)


def _rms_matmul(x, g, w, *, tm=1024, tn=2048):
    T, D = x.shape
    N = w.shape[1]
    assert T % tm == 0 and N % tn == 0
    return pl.pallas_call(
        _rms_matmul_body,
        grid=(T // tm, N // tn),
        in_specs=[
            pl.BlockSpec((tm, D), lambda i, j: (i, 0)),
            pl.BlockSpec((1, D), lambda i, j: (0, 0)),
            pl.BlockSpec((D, tn), lambda i, j: (0, j)),
        ],
        out_specs=pl.BlockSpec((tm, tn), lambda i, j: (i, j)),
        out_shape=jax.ShapeDtypeStruct((T, N), BF16),
        scratch_shapes=[pltpu.VMEM((tm, D), BF16)],
        compiler_params=_params("parallel", "arbitrary"),
        name="rms_in_proj",
    )(x, g, w)


def _gated_out_body(o_ref, gate_ref, w_ref, x_ref, y_ref):
    og = (o_ref[...].astype(F32) * _silu(gate_ref[...].astype(F32))).astype(BF16)
    y_ref[...] = x_ref[...] + _dot(og, w_ref[...])


def _gated_out(o, u, gate_blk, w_out, x, *, tm=512):
    T, E = o.shape
    D = x.shape[1]
    assert T % tm == 0
    return pl.pallas_call(
        _gated_out_body,
        grid=(T // tm,),
        in_specs=[
            pl.BlockSpec((tm, E), lambda i: (i, 0)),
            pl.BlockSpec((tm, E), lambda i: (i, gate_blk)),
            pl.BlockSpec((E, D), lambda i: (0, 0)),
            pl.BlockSpec((tm, D), lambda i: (i, 0)),
        ],
        out_specs=pl.BlockSpec((tm, D), lambda i: (i, 0)),
        out_shape=jax.ShapeDtypeStruct((T, D), F32),
        compiler_params=_params("parallel"),
        name="gated_out_proj",
    )(o, u, w_out, x)


def _conv_body(a_ref, b_ref, gate_ref, ah_ref, bh_ref, x_ref, wdw_ref, bdw_ref, lng_ref, lnb_ref,
               wout_ref, y_ref, sh_ref, acc_ref, *, tm, E, rows, cols):
    i = pl.program_id(1)
    off = CONV_HALO - (CONV_K - 1)
    n_sh = tm + CONV_HALO - SUBLANES
    for c0 in range(0, E, cols):
        cs = slice(c0, c0 + cols)
        vh = ah_ref[:, cs].astype(F32) * _sigmoid(bh_ref[:, cs].astype(F32))
        sh_ref[0, 0:CONV_HALO, :] = jnp.where(i > 0, vh, 0.0)
        sh_ref[0, CONV_HALO:, :] = a_ref[:, cs].astype(F32) * _sigmoid(b_ref[:, cs].astype(F32))
        for s in range(1, SUBLANES):
            sh_ref[s, 0:n_sh, :] = sh_ref[0, s:s + n_sh, :]
        for r0 in range(0, tm, rows):
            acc = jnp.broadcast_to(bdw_ref[:, cs], (rows, cols))
            for k in range(CONV_K):
                s, a8 = (off + k) % SUBLANES, (off + k) // SUBLANES * SUBLANES
                wk = jnp.concatenate([wdw_ref[k, :, cs]] * (rows // SUBLANES), axis=0)
                acc = acc + wk * sh_ref[s, r0 + a8:r0 + a8 + rows, :]
            acc_ref[r0:r0 + rows, cs] = acc

    c = acc_ref[...]
    mu = jnp.mean(c, axis=-1, keepdims=True)
    d = c - mu
    var = jnp.mean(d * d, axis=-1, keepdims=True)
    ln = d * lax.rsqrt(var + EPS) * lng_ref[...] + lnb_ref[...]
    og = (_silu(ln) * _silu(gate_ref[...].astype(F32))).astype(BF16)
    y_ref[...] = x_ref[...] + _dot(og, wout_ref[...])


def _conv_layer(x, g, w_in, w_dw, b_dw, ln_g, ln_b, w_out, *, B, S, tm=256):
    T, D = x.shape
    E = w_out.shape[0]
    u = _rms_matmul(x, g, w_in)
    nS = S // tm
    hpb = tm // CONV_HALO

    def tile(col):
        return pl.BlockSpec((tm, E), lambda b, i: (b * nS + i, col))

    def halo(col):
        return pl.BlockSpec((CONV_HALO, E), lambda b, i: (jnp.maximum((b * nS + i) * hpb - 1, 0), col))

    def row(n):
        return pl.BlockSpec((1, n), lambda b, i: (0, 0))

    cols = 512
    body = functools.partial(_conv_body, tm=tm, E=E, rows=32, cols=cols)
    return pl.pallas_call(
        body,
        grid=(B, nS),
        in_specs=[tile(0), tile(1), tile(2), halo(0), halo(1),
                  pl.BlockSpec((tm, D), lambda b, i: (b * nS + i, 0)),
                  pl.BlockSpec((CONV_K, SUBLANES, E), lambda b, i: (0, 0, 0)),
                  row(E), row(E), row(E),
                  pl.BlockSpec((E, D), lambda b, i: (0, 0))],
        out_specs=pl.BlockSpec((tm, D), lambda b, i: (b * nS + i, 0)),
        out_shape=jax.ShapeDtypeStruct((T, D), F32),
        scratch_shapes=[pltpu.VMEM((SUBLANES, tm + CONV_HALO, cols), F32), pltpu.VMEM((tm, E), F32)],
        compiler_params=_params("parallel", "parallel"),
        name="conv_mixer",
    )(u, u, u, u, u, x, jnp.broadcast_to(w_dw[:, None, :], (CONV_K, SUBLANES, E)), b_dw, ln_g, ln_b, w_out)


def _fox_cum_body(x_ref, g_ref, wf_ref, bf_ref, o_ref, carry_ref, *, tc, H):
    @pl.when(pl.program_id(1) == 0)
    def _():
        carry_ref[...] = jnp.zeros_like(carry_ref)

    hn = _rms(x_ref[...], g_ref[...]).astype(BF16)
    log_f = _log_sigmoid(_dot(hn, wf_ref[...]) + bf_ref[...])
    r = lax.broadcasted_iota(jnp.int32, (tc, tc), 0)
    c = lax.broadcasted_iota(jnp.int32, (tc, tc), 1)
    tril = jnp.where(c <= r, 1.0, 0.0).astype(BF16)
    cum = _dot_01(tril, log_f) + carry_ref[...]
    carry_ref[...] = cum[tc - 1:tc, :]
    o_ref[0] = cum.T[0:H, :]


def _fox_cum(x, g, wf, bfp, *, B, S, H, tc=512):
    T, D = x.shape
    nS = S // tc
    body = functools.partial(_fox_cum_body, tc=tc, H=H)
    return pl.pallas_call(
        body,
        grid=(B, nS),
        in_specs=[
            pl.BlockSpec((tc, D), lambda b, i: (b * nS + i, 0)),
            pl.BlockSpec((1, D), lambda b, i: (0, 0)),
            pl.BlockSpec((D, LANES), lambda b, i: (0, 0)),
            pl.BlockSpec((1, LANES), lambda b, i: (0, 0)),
        ],
        out_specs=pl.BlockSpec((1, H, tc), lambda b, i: (b, 0, i)),
        out_shape=jax.ShapeDtypeStruct((B, H, S), F32),
        scratch_shapes=[pltpu.VMEM((1, LANES), F32)],
        compiler_params=_params("parallel", "arbitrary"),
        name="fox_forget_cumsum",
    )(x, g, wf, bfp)


def _fox_attn_body(q_ref, k_ref, v_ref, c_ref, o_ref, vt_ref, ka_ref, cc_ref, s0_ref, s1_ref, qa_ref, m_ref,
                   acc_ref, *, S, tq, tk, ch, scale):
    dh = q_ref.shape[1]
    log2e = 1.4426950408889634
    assert tq == 2 * tk, "a query tile spans exactly two kv blocks (pairwise pipelined loop)"
    lane = lax.broadcasted_iota(jnp.int32, (1, LANES), 1)

    def split3(x):
        hi = x.astype(BF16).astype(F32)
        mid = (x - hi).astype(BF16).astype(F32)
        lo = (x - hi - mid).astype(BF16).astype(F32)
        return hi, mid, lo

    for n0 in range(S // ch):
        rows = slice(n0 * ch, (n0 + 1) * ch)
        vt = v_ref[rows, :].astype(F32).T.astype(BF16)
        for s0 in range(ch // tk):
            vt_ref[n0 * (ch // tk) + s0, 0:dh, :] = vt[:, s0 * tk:(s0 + 1) * tk]
            vt_ref[n0 * (ch // tk) + s0, dh:, :] = jnp.ones((ONES_ROWS, tk), BF16)
        crow = c_ref[0, 0, n0:n0 + 1, :] * log2e
        cc = jnp.broadcast_to(crow, (LANES, ch)).T
        cc_ref[rows, :] = cc
        hi, mid, lo = split3(cc)
        aug = jnp.where(lane == 0, -hi, jnp.where(lane == 1, -mid, jnp.where(lane == 2, -lo,
                        jnp.where(lane < 6, 1.0, 0.0))))
        ka_ref[rows, 0:dh] = k_ref[rows, :]
        ka_ref[rows, dh:] = aug.astype(BF16)

    def q_tile(i, _):
        q0 = pl.multiple_of(i * tq, tq)
        qw = (q_ref[pl.ds(q0, tq), :].astype(F32) * (scale * log2e)).astype(BF16)
        hi, mid, lo = split3(cc_ref[pl.ds(q0, 1), :])
        aug = jnp.where(lane < 3, 1.0, jnp.where(lane == 3, hi, jnp.where(lane == 4, mid,
                        jnp.where(lane == 5, lo, 0.0))))
        qa_ref[:, 0:dh] = qw
        qa_ref[:, dh:] = jnp.broadcast_to(aug, (tq, LANES)).astype(BF16)
        m_ref[...] = jnp.full((1, tq), -jnp.inf, F32)
        acc_ref[...] = jnp.zeros_like(acc_ref)

        def logits(j, s_ref, lo_q):
            kv0 = pl.multiple_of(j * tk, tk)
            s_ref[:, lo_q:] = _dot_nt(ka_ref[pl.ds(kv0, tk), :], qa_ref[lo_q:, :])

        def consume(j, s_ref, diag, lo_q):
            st = s_ref[:, lo_q:]
            if diag is not None:
                r = lax.broadcasted_iota(jnp.int32, st.shape, 0) + diag * tk
                c = lax.broadcasted_iota(jnp.int32, st.shape, 1) + lo_q
                st = jnp.where(r <= c, st, -jnp.inf)
            m_old = m_ref[:, lo_q:]
            m_new = jnp.maximum(m_old, jnp.max(st, axis=0, keepdims=True))
            alpha = jnp.exp2(m_old - m_new)
            p = jnp.exp2(st - m_new)
            m_ref[:, lo_q:] = m_new
            acc_ref[:, lo_q:] = alpha * acc_ref[:, lo_q:] + _dot(vt_ref[j], p.astype(BF16))

        def pairs(n_pairs):
            def body(it, _):
                for u in range(n_pairs):
                    j = 2 * (n_pairs * it + u)
                    logits(j + 1, s1_ref, 0)
                    consume(j, s0_ref, None, 0)
                    logits(j + 2, s0_ref, 0)
                    consume(j + 1, s1_ref, None, 0)
                return 0
            return body

        logits(0, s0_ref, 0)
        lax.fori_loop(0, i // 2, pairs(2), 0)
        lax.fori_loop(2 * (i // 2), i, pairs(1), 0)
        logits(2 * i + 1, s1_ref, tk)
        consume(2 * i, s0_ref, 0, 0)
        consume(2 * i + 1, s1_ref, 1, tk)
        o_ref[pl.ds(q0, tq), :] = (acc_ref[0:dh, :] / acc_ref[dh:dh + 1, :]).T.astype(o_ref.dtype)
        return 0

    lax.fori_loop(0, S // tq, q_tile, 0)


def _fox_attention(u, cum, *, B, S, H, E, tq=1024, tk=512, ch=512):
    T = u.shape[0]
    dh = FOX_HEAD_DIM
    cum4 = cum.reshape(B, H, S // ch, ch)
    body = functools.partial(_fox_attn_body, S=S, tq=tq, tk=tk, ch=ch, scale=dh ** -0.5)
    return pl.pallas_call(
        body,
        grid=(B, H),
        in_specs=[
            pl.BlockSpec((S, dh), lambda b, h: (b, h)),
            pl.BlockSpec((S, dh), lambda b, h: (b, H + h)),
            pl.BlockSpec((S, dh), lambda b, h: (b, 2 * H + h)),
            pl.BlockSpec((1, 1, S // ch, ch), lambda b, h: (b, h, 0, 0)),
        ],
        out_specs=pl.BlockSpec((S, dh), lambda b, h: (b, h)),
        out_shape=jax.ShapeDtypeStruct((T, E), BF16),
        scratch_shapes=[pltpu.VMEM((S // tk, dh + ONES_ROWS, tk), BF16), pltpu.VMEM((S, 2 * dh), BF16),
                        pltpu.VMEM((S, LANES), F32), pltpu.VMEM((tk, tq), F32), pltpu.VMEM((tk, tq), F32),
                        pltpu.VMEM((tq, 2 * dh), BF16), pltpu.VMEM((1, tq), F32),
                        pltpu.VMEM((dh + ONES_ROWS, tq), F32)],
        compiler_params=_params("parallel", "parallel"),
        name="fox_attention",
    )(u, u, u, cum4)


def _fox_layer(x, g, w_in, b_f, w_out, *, B, S):
    E = w_out.shape[0]
    H = E // FOX_HEAD_DIM
    w_main = w_in[:, :3 * E]
    w_f = w_in[:, 3 * E:3 * E + H]
    w_gate = w_in[:, 3 * E + H:]
    w_cat = jnp.concatenate([w_main, w_gate], axis=1).astype(BF16)
    wf_pad = jnp.pad(w_f, ((0, 0), (0, LANES - H))).astype(BF16)
    bf_pad = jnp.pad(b_f, (0, LANES - H)).reshape(1, LANES)
    u = _rms_matmul(x, g, w_cat)
    cum = _fox_cum(x, g, wf_pad, bf_pad, B=B, S=S, H=H)
    o = _fox_attention(u, cum, B=B, S=S, H=H, E=E)
    return _gated_out(o, u, 3, w_out.astype(BF16), x)


def _gla_gate_body(x_ref, g_ref, w1_ref, w2_ref, b_ref, o_ref):
    hn = _rms(x_ref[...], g_ref[...]).astype(BF16)
    low = _dot(hn, w1_ref[...]).astype(BF16)
    o_ref[...] = _log_sigmoid(_dot(low, w2_ref[...]) + b_ref[...]) * (1.0 / GLA_GATE_NORM)


def _gla_gate(x, g, w1p, w2p, b_g, *, tm=512):
    T, D = x.shape
    KD = w2p.shape[1]
    return pl.pallas_call(
        _gla_gate_body,
        grid=(T // tm,),
        in_specs=[
            pl.BlockSpec((tm, D), lambda i: (i, 0)),
            pl.BlockSpec((1, D), lambda i: (0, 0)),
            pl.BlockSpec((D, LANES), lambda i: (0, 0)),
            pl.BlockSpec((LANES, KD), lambda i: (0, 0)),
            pl.BlockSpec((1, KD), lambda i: (0, 0)),
        ],
        out_specs=pl.BlockSpec((tm, KD), lambda i: (i, 0)),
        out_shape=jax.ShapeDtypeStruct((T, KD), F32),
        compiler_params=_params("parallel"),
        name="gla_log_decay",
    )(x, g, w1p, w2p, b_g)


def _gla_chunk_body(q_ref, k_ref, v_ref, g_ref, gn_ref, o_ref, st_ref, *, tb, dk, dv, scale):
    C = GLA_CHUNK

    @pl.when(pl.program_id(1) == 0)
    def _():
        st_ref[...] = jnp.zeros_like(st_ref)

    r = lax.broadcasted_iota(jnp.int32, (tb, tb), 0)
    c = lax.broadcasted_iota(jnp.int32, (tb, tb), 1)
    shift = C.bit_length() - 1
    same = jnp.right_shift(r, shift) == jnp.right_shift(c, shift)
    tril_blk = jnp.where(jnp.logical_and(same, c <= r), 1.0, 0.0).astype(BF16)
    b_all = _dot_01(tril_blk, g_ref[...])

    rc = lax.broadcasted_iota(jnp.int32, (C, C), 0)
    cc = lax.broadcasted_iota(jnp.int32, (C, C), 1)
    causal = cc <= rc

    for n in range(tb // C):
        rows = slice(n * C, (n + 1) * C)
        for h in range(GLA_HEADS):
            ks_, vs_ = slice(h * dk, (h + 1) * dk), slice(h * dv, (h + 1) * dv)
            qc = q_ref[rows, ks_].astype(F32) * scale
            kc = k_ref[rows, ks_].astype(F32)
            vc = v_ref[rows, vs_]
            b = b_all[rows, ks_]
            b_last = b[C - 1:C, :]
            b_mid = b[C // 2:C // 2 + 1, :]
            st = st_ref[h]
            o_inter = _dot_nt((qc * jnp.exp(b)).astype(BF16), st.astype(BF16))
            qs = (qc * jnp.exp(b - b_mid)).astype(BF16)
            ks = (kc * jnp.exp(b_mid - b)).astype(BF16)
            scores = jnp.where(causal, _dot_nt(qs, ks), 0.0)
            o = o_inter + _dot(scores.astype(BF16), vc)
            k_dec = (kc * jnp.exp(b_last - b)).astype(BF16)
            st_ref[h] = st * jnp.exp(b_last) + _dot_tn(vc, k_dec)
            o_ref[rows, vs_] = _rms(o, gn_ref[...]).astype(o_ref.dtype)


def _gla_chunks(u, gdec, gn_g, *, B, S, E, tb=256):
    T = u.shape[0]
    Hh = GLA_HEADS
    KD = gdec.shape[1]
    dk, dv = KD // Hh, E // Hh
    nS = S // tb
    body = functools.partial(_gla_chunk_body, tb=tb, dk=dk, dv=dv, scale=dk ** -0.5)
    return pl.pallas_call(
        body,
        grid=(B, nS),
        in_specs=[
            pl.BlockSpec((tb, KD), lambda b, i: (b * nS + i, 0)),
            pl.BlockSpec((tb, KD), lambda b, i: (b * nS + i, 1)),
            pl.BlockSpec((tb, E), lambda b, i: (b * nS + i, (2 * KD) // E)),
            pl.BlockSpec((tb, KD), lambda b, i: (b * nS + i, 0)),
            pl.BlockSpec((1, dv), lambda b, i: (0, 0)),
        ],
        out_specs=pl.BlockSpec((tb, E), lambda b, i: (b * nS + i, 0)),
        out_shape=jax.ShapeDtypeStruct((T, E), BF16),
        scratch_shapes=[pltpu.VMEM((Hh, dv, dk), F32)],
        compiler_params=_params("parallel", "arbitrary"),
        name="gla_chunks",
    )(u, u, u, gdec, gn_g)


def _gla_layer(x, g, w_in, w_g1, w_g2, b_g, gn_g, w_out, *, B, S):
    E = w_out.shape[0]
    KD = w_g2.shape[1]
    rank = w_g1.shape[1]
    u = _rms_matmul(x, g, w_in.astype(BF16))
    w1p = jnp.pad(w_g1, ((0, 0), (0, LANES - rank))).astype(BF16)
    w2p = jnp.pad(w_g2, ((0, LANES - rank), (0, 0))).astype(BF16)
    gdec = _gla_gate(x, g, w1p, w2p, b_g.reshape(1, KD))
    o = _gla_chunks(u, gdec, gn_g.reshape(1, -1), B=B, S=S, E=E)
    return _gated_out(o, u, (2 * KD + E) // E, w_out.astype(BF16), x)


def _lru_body(xb_ref, xh_ref, gate_ref, x_ref, wc_ref, bc_ref, wax_ref, ba_ref, bx_ref, lam_ref,
              wout_ref, fg_ref, y_ref, buf_ref, a_ref, b_ref, h_ref, carry_ref, *, tm, E):
    i = pl.program_id(1)

    @pl.when(i == 0)
    def _():
        carry_ref[...] = jnp.zeros_like(carry_ref)

    buf_ref[0:LRU_HALO, :] = jnp.where(i > 0, xh_ref[...].astype(F32), 0.0)
    buf_ref[LRU_HALO:, :] = xb_ref[...].astype(F32)
    off = LRU_HALO - (LRU_CONV_K - 1)
    xc = jnp.broadcast_to(bc_ref[...], (tm, E))
    for k in range(LRU_CONV_K):
        xc = xc + wc_ref[k:k + 1, :] * buf_ref[off + k:off + k + tm, :]

    lam = lam_ref[...]
    sp = jnp.maximum(-lam, 0.0) + jnp.log(1.0 + jnp.exp(-jnp.abs(lam)))
    rate = sp * (-LRU_C * 1.4426950408889634)
    W = LRU_BLOCK_W
    for n in range(E // W):
        sl = slice(n * W, (n + 1) * W)
        xn = xc[:, sl]
        pre = _dot(xn.astype(BF16), wax_ref[n])
        r = _sigmoid(pre[:, :W] + ba_ref[:, sl])
        ig = _sigmoid(pre[:, W:] + bx_ref[:, sl])
        a = jnp.exp2(r * rate[:, sl])
        beta = jnp.sqrt(1.0 - a * a)
        a_ref[:, sl] = a
        b_ref[:, sl] = beta * (ig * xn)

    def step(t, h):
        h = a_ref[pl.ds(t, 1), :] * h + b_ref[pl.ds(t, 1), :]
        h_ref[pl.ds(t, 1), :] = h
        return h

    carry_ref[...] = lax.fori_loop(0, tm, step, carry_ref[...], unroll=8)

    og = (h_ref[...] * _silu(gate_ref[...].astype(F32))).astype(BF16)
    y = x_ref[...] + _dot(og, wout_ref[...])
    y_ref[...] = _rms(y, fg_ref[...])


def _lru_layer(x, g, w_in, w_conv, b_conv, w_a, b_a, w_x, b_x, lam, w_out, final_g, *, B, S, tm=256):
    T, D = x.shape
    E = w_out.shape[0]
    u = _rms_matmul(x, g, w_in.astype(BF16))
    wax = jnp.concatenate([w_a, w_x], axis=-1).astype(BF16)
    nS = S // tm
    hpb = tm // LRU_HALO

    def row(n):
        return pl.BlockSpec((1, n), lambda b, i: (0, 0))

    body = functools.partial(_lru_body, tm=tm, E=E)
    return pl.pallas_call(
        body,
        grid=(B, nS),
        in_specs=[
            pl.BlockSpec((tm, E), lambda b, i: (b * nS + i, 0)),
            pl.BlockSpec((LRU_HALO, E), lambda b, i: (jnp.maximum((b * nS + i) * hpb - 1, 0), 0)),
            pl.BlockSpec((tm, E), lambda b, i: (b * nS + i, 1)),
            pl.BlockSpec((tm, D), lambda b, i: (b * nS + i, 0)),
            pl.BlockSpec((LRU_CONV_K, E), lambda b, i: (0, 0)),
            row(E),
            pl.BlockSpec(wax.shape, lambda b, i: (0, 0, 0)),
            row(E), row(E), row(E),
            pl.BlockSpec((E, D), lambda b, i: (0, 0)),
            row(D),
        ],
        out_specs=pl.BlockSpec((tm, D), lambda b, i: (b * nS + i, 0)),
        out_shape=jax.ShapeDtypeStruct((T, D), F32),
        scratch_shapes=[pltpu.VMEM((tm + LRU_HALO, E), F32), pltpu.VMEM((tm, E), F32),
                        pltpu.VMEM((tm, E), F32), pltpu.VMEM((tm, E), F32), pltpu.VMEM((1, E), F32)],
        compiler_params=_params("parallel", "arbitrary"),
        name="rglru_mixer",
    )(u, u, u, x, w_conv, b_conv.reshape(1, E), wax, b_a.reshape(1, E), b_x.reshape(1, E),
      lam.reshape(1, E), w_out.astype(BF16), final_g.reshape(1, D))


def kernel(x, norm_g, final_g, conv_w_in, conv_w_dw, conv_b_dw, conv_ln_g, conv_ln_b, conv_w_out, fox_w_in, fox_b_f, fox_w_out, gla_w_in, gla_w_g1, gla_w_g2, gla_b_g, gla_gn_g, gla_w_out, lru_w_in, lru_w_conv, lru_b_conv, lru_w_a, lru_b_a, lru_w_x, lru_b_x, lru_lam, lru_w_out):
    B, S, D = x.shape
    assert norm_g.shape[0] == 4 and conv_w_in.shape[0] == 1, "one layer per mixer"
    E = conv_w_out.shape[1]
    h = x.reshape(B * S, D)
    ng = norm_g.reshape(4, 1, D)
    h = _conv_layer(h, ng[0], conv_w_in[0].astype(BF16), conv_w_dw[0], conv_b_dw[0].reshape(1, E),
                    conv_ln_g[0].reshape(1, E), conv_ln_b[0].reshape(1, E), conv_w_out[0].astype(BF16), B=B, S=S)
    h = _fox_layer(h, ng[1], fox_w_in[0], fox_b_f[0], fox_w_out[0], B=B, S=S)
    h = _gla_layer(h, ng[2], gla_w_in[0], gla_w_g1[0], gla_w_g2[0], gla_b_g[0], gla_gn_g[0], gla_w_out[0], B=B, S=S)
    h = _lru_layer(h, ng[3], lru_w_in[0], lru_w_conv[0], lru_b_conv[0], lru_w_a[0], lru_b_a[0], lru_w_x[0],
                   lru_b_x[0], lru_lam[0], lru_w_out[0], final_g, B=B, S=S)
    return h.reshape(B, S, D)
```

```python
import functools

import jax
import jax.numpy as jnp
from jax import lax
from jax.experimental import pallas as pl
from jax.experimental.pallas import tpu as pltpu

F32 = jnp.float32
BF16 = jnp.bfloat16
EPS = 1e-6

V7X_VMEM_BYTES = 64 * 1024 * 1024
VMEM_LIMIT_BYTES = V7X_VMEM_BYTES - 16 * 1024 * 1024
LANES = 128
SUBLANES = 8

CONV_K = 31
CONV_HALO = 32
FOX_HEAD_DIM = 128
ONES_ROWS = 16
UNDERFLOW_BITS = 176.0
GLA_HEADS = 4
GLA_CHUNK = 64
GLA_GATE_NORM = 16.0
LRU_CONV_K = 4
LRU_HALO = 16
LRU_BLOCK_W = 128
LRU_C = 8.0


def _params(*sem):
    return pltpu.CompilerParams(dimension_semantics=sem, vmem_limit_bytes=VMEM_LIMIT_BYTES)


def _sigmoid(x):
    return 1.0 / (1.0 + jnp.exp(-x))


def _silu(x):
    return x * _sigmoid(x)


def _log_sigmoid(x):
    return jnp.minimum(x, 0.0) - jnp.log(1.0 + jnp.exp(-jnp.abs(x)))


def _rms(x, g):
    return x * lax.rsqrt(jnp.mean(x * x, axis=-1, keepdims=True) + EPS) * g


def _dot(a, b):
    return jnp.dot(a, b, preferred_element_type=F32)


def _dot_nt(a, b):
    return lax.dot_general(a, b, (((1,), (1,)), ((), ())), preferred_element_type=F32)


def _dot_tn(a, b):
    return lax.dot_general(a, b, (((0,), (0,)), ((), ())), preferred_element_type=F32)


def _dot_01(m01, x):
    x1 = x.astype(BF16)
    r1 = x - x1.astype(F32)
    x2 = r1.astype(BF16)
    x3 = (r1 - x2.astype(F32)).astype(BF16)
    return _dot(m01, x1) + _dot(m01, x2) + _dot(m01, x3)


def _rms_matmul_body(x_ref, g_ref, w_ref, o_ref, hn_ref):
    @pl.when(pl.program_id(1) == 0)
    def _():
        hn_ref[...] = _rms(x_ref[...], g_ref[...]).astype(BF16)

    o_ref[...] = _dot(hn_ref[...], w_ref[...]).astype(o_ref.dtype)


def _rms_matmul(x, g, w, *, tm=1024, tn=2048):
    T, D = x.shape
    N = w.shape[1]
    assert T % tm == 0 and N % tn == 0
    return pl.pallas_call(
        _rms_matmul_body,
        grid=(T // tm, N // tn),
        in_specs=[
            pl.BlockSpec((tm, D), lambda i, j: (i, 0)),
            pl.BlockSpec((1, D), lambda i, j: (0, 0)),
            pl.BlockSpec((D, tn), lambda i, j: (0, j)),
        ],
        out_specs=pl.BlockSpec((tm, tn), lambda i, j: (i, j)),
        out_shape=jax.ShapeDtypeStruct((T, N), BF16),
        scratch_shapes=[pltpu.VMEM((tm, D), BF16)],
        compiler_params=_params("parallel", "arbitrary"),
        name="rms_in_proj",
    )(x, g, w)


def _gated_out_body(o_ref, gate_ref, w_ref, x_ref, y_ref):
    og = (o_ref[...].astype(F32) * _silu(gate_ref[...].astype(F32))).astype(BF16)
    y_ref[...] = x_ref[...] + _dot(og, w_ref[...])


def _gated_out(o, u, gate_blk, w_out, x, *, tm=512):
    T, E = o.shape
    D = x.shape[1]
    assert T % tm == 0
    return pl.pallas_call(
        _gated_out_body,
        grid=(T // tm,),
        in_specs=[
            pl.BlockSpec((tm, E), lambda i: (i, 0)),
            pl.BlockSpec((tm, E), lambda i: (i, gate_blk)),
            pl.BlockSpec((E, D), lambda i: (0, 0)),
            pl.BlockSpec((tm, D), lambda i: (i, 0)),
        ],
        out_specs=pl.BlockSpec((tm, D), lambda i: (i, 0)),
        out_shape=jax.ShapeDtypeStruct((T, D), F32),
        compiler_params=_params("parallel"),
        name="gated_out_proj",
    )(o, u, w_out, x)


def _conv_body(a_ref, b_ref, gate_ref, ah_ref, bh_ref, x_ref, wdw_ref, bdw_ref, lng_ref, lnb_ref,
               wout_ref, y_ref, sh_ref, acc_ref, *, tm, E, rows, cols):
    i = pl.program_id(1)
    off = CONV_HALO - (CONV_K - 1)
    n_sh = tm + CONV_HALO - SUBLANES
    for c0 in range(0, E, cols):
        cs = slice(c0, c0 + cols)
        vh = ah_ref[:, cs].astype(F32) * _sigmoid(bh_ref[:, cs].astype(F32))
        sh_ref[0, 0:CONV_HALO, :] = jnp.where(i > 0, vh, 0.0)
        sh_ref[0, CONV_HALO:, :] = a_ref[:, cs].astype(F32) * _sigmoid(b_ref[:, cs].astype(F32))
        for s in range(1, SUBLANES):
            sh_ref[s, 0:n_sh, :] = sh_ref[0, s:s + n_sh, :]
        for r0 in range(0, tm, rows):
            acc = jnp.broadcast_to(bdw_ref[:, cs], (rows, cols))
            for k in range(CONV_K):
                s, a8 = (off + k) % SUBLANES, (off + k) // SUBLANES * SUBLANES
                wk = jnp.concatenate([wdw_ref[k, :, cs]] * (rows // SUBLANES), axis=0)
                acc = acc + wk * sh_ref[s, r0 + a8:r0 + a8 + rows, :]
            acc_ref[r0:r0 + rows, cs] = acc

    c = acc_ref[...]
    mu = jnp.mean(c, axis=-1, keepdims=True)
    d = c - mu
    var = jnp.mean(d * d, axis=-1, keepdims=True)
    ln = d * lax.rsqrt(var + EPS) * lng_ref[...] + lnb_ref[...]
    og = (_silu(ln) * _silu(gate_ref[...].astype(F32))).astype(BF16)
    y_ref[...] = x_ref[...] + _dot(og, wout_ref[...])


def _conv_layer(x, g, w_in, w_dw, b_dw, ln_g, ln_b, w_out, *, B, S, tm=256):
    T, D = x.shape
    E = w_out.shape[0]
    u = _rms_matmul(x, g, w_in)
    nS = S // tm
    hpb = tm // CONV_HALO

    def tile(col):
        return pl.BlockSpec((tm, E), lambda b, i: (b * nS + i, col))

    def halo(col):
        return pl.BlockSpec((CONV_HALO, E), lambda b, i: (jnp.maximum((b * nS + i) * hpb - 1, 0), col))

    def row(n):
        return pl.BlockSpec((1, n), lambda b, i: (0, 0))

    cols = 512
    body = functools.partial(_conv_body, tm=tm, E=E, rows=32, cols=cols)
    return pl.pallas_call(
        body,
        grid=(B, nS),
        in_specs=[tile(0), tile(1), tile(2), halo(0), halo(1),
                  pl.BlockSpec((tm, D), lambda b, i: (b * nS + i, 0)),
                  pl.BlockSpec((CONV_K, SUBLANES, E), lambda b, i: (0, 0, 0)),
                  row(E), row(E), row(E),
                  pl.BlockSpec((E, D), lambda b, i: (0, 0))],
        out_specs=pl.BlockSpec((tm, D), lambda b, i: (b * nS + i, 0)),
        out_shape=jax.ShapeDtypeStruct((T, D), F32),
        scratch_shapes=[pltpu.VMEM((SUBLANES, tm + CONV_HALO, cols), F32), pltpu.VMEM((tm, E), F32)],
        compiler_params=_params("parallel", "parallel"),
        name="conv_mixer",
    )(u, u, u, u, u, x, jnp.broadcast_to(w_dw[:, None, :], (CONV_K, SUBLANES, E)), b_dw, ln_g, ln_b, w_out)


def _fox_cum_body(x_ref, g_ref, wf_ref, bf_ref, o_ref, carry_ref, *, tc, H):
    @pl.when(pl.program_id(1) == 0)
    def _():
        carry_ref[...] = jnp.zeros_like(carry_ref)

    hn = _rms(x_ref[...], g_ref[...]).astype(BF16)
    log_f = _log_sigmoid(_dot(hn, wf_ref[...]) + bf_ref[...])
    r = lax.broadcasted_iota(jnp.int32, (tc, tc), 0)
    c = lax.broadcasted_iota(jnp.int32, (tc, tc), 1)
    tril = jnp.where(c <= r, 1.0, 0.0).astype(BF16)
    cum = _dot_01(tril, log_f) + carry_ref[...]
    carry_ref[...] = cum[tc - 1:tc, :]
    o_ref[0] = cum.T[0:H, :]


def _fox_cum(x, g, wf, bfp, *, B, S, H, tc=512):
    T, D = x.shape
    nS = S // tc
    body = functools.partial(_fox_cum_body, tc=tc, H=H)
    return pl.pallas_call(
        body,
        grid=(B, nS),
        in_specs=[
            pl.BlockSpec((tc, D), lambda b, i: (b * nS + i, 0)),
            pl.BlockSpec((1, D), lambda b, i: (0, 0)),
            pl.BlockSpec((D, LANES), lambda b, i: (0, 0)),
            pl.BlockSpec((1, LANES), lambda b, i: (0, 0)),
        ],
        out_specs=pl.BlockSpec((1, H, tc), lambda b, i: (b, 0, i)),
        out_shape=jax.ShapeDtypeStruct((B, H, S), F32),
        scratch_shapes=[pltpu.VMEM((1, LANES), F32)],
        compiler_params=_params("parallel", "arbitrary"),
        name="fox_forget_cumsum",
    )(x, g, wf, bfp)


def _fox_attn_body(q_ref, k_ref, v_ref, c_ref, o_ref, vt_ref, ka_ref, cc_ref, s0_ref, s1_ref, qa_ref, m_ref,
                   acc_ref, *, S, tq, tk, ch, scale):
    dh = q_ref.shape[1]
    log2e = 1.4426950408889634
    assert tq == 2 * tk, "a query tile spans exactly two kv blocks (pairwise pipelined loop)"
    lane = lax.broadcasted_iota(jnp.int32, (1, LANES), 1)

    def split3(x):
        hi = x.astype(BF16).astype(F32)
        mid = (x - hi).astype(BF16).astype(F32)
        lo = (x - hi - mid).astype(BF16).astype(F32)
        return hi, mid, lo

    kn2 = jnp.zeros((1, 1), F32)
    for n0 in range(S // ch):
        rows = slice(n0 * ch, (n0 + 1) * ch)
        kf = k_ref[rows, :].astype(F32)
        kn2 = jnp.maximum(kn2, jnp.max(jnp.sum(kf * kf, axis=1, keepdims=True), axis=0, keepdims=True))
        vt = v_ref[rows, :].astype(F32).T.astype(BF16)
        for s0 in range(ch // tk):
            vt_ref[n0 * (ch // tk) + s0, 0:dh, :] = vt[:, s0 * tk:(s0 + 1) * tk]
            vt_ref[n0 * (ch // tk) + s0, dh:, :] = jnp.ones((ONES_ROWS, tk), BF16)
        crow = c_ref[0, 0, n0:n0 + 1, :] * log2e
        cc = jnp.broadcast_to(crow, (LANES, ch)).T
        cc_ref[rows, :] = cc
        hi, mid, lo = split3(cc)
        aug = jnp.where(lane == 0, -hi, jnp.where(lane == 1, -mid, jnp.where(lane == 2, -lo,
                        jnp.where(lane < 6, 1.0, 0.0))))
        ka_ref[rows, 0:dh] = k_ref[rows, :]
        ka_ref[rows, dh:] = aug.astype(BF16)
    c_end = cc_ref[pl.ds(tk - 1, S // tk, stride=tk), :]

    def q_tile(i, _):
        q0 = pl.multiple_of(i * tq, tq)
        qw = (q_ref[pl.ds(q0, tq), :].astype(F32) * (scale * log2e)).astype(BF16)
        qf = qw.astype(F32)
        qn2 = jnp.max(jnp.sum(qf * qf, axis=1, keepdims=True), axis=0, keepdims=True)
        reach = cc_ref[pl.ds(q0, 1), :] + (UNDERFLOW_BITS + 2.0 * jnp.sqrt(qn2 * kn2))
        n_dead = jnp.sum(jnp.where(c_end > reach, 1, 0)[:, 0:1])
        p0 = jnp.minimum(n_dead // 2, i)
        hi, mid, lo = split3(cc_ref[pl.ds(q0, 1), :])
        aug = jnp.where(lane < 3, 1.0, jnp.where(lane == 3, hi, jnp.where(lane == 4, mid,
                        jnp.where(lane == 5, lo, 0.0))))
        qa_ref[:, 0:dh] = qw
        qa_ref[:, dh:] = jnp.broadcast_to(aug, (tq, LANES)).astype(BF16)
        m_ref[...] = jnp.full((1, tq), -jnp.inf, F32)
        acc_ref[...] = jnp.zeros_like(acc_ref)

        def logits(j, s_ref, lo_q):
            kv0 = pl.multiple_of(j * tk, tk)
            s_ref[:, lo_q:] = _dot_nt(ka_ref[pl.ds(kv0, tk), :], qa_ref[lo_q:, :])

        def consume(j, s_ref, diag, lo_q):
            st = s_ref[:, lo_q:]
            if diag is not None:
                r = lax.broadcasted_iota(jnp.int32, st.shape, 0) + diag * tk
                c = lax.broadcasted_iota(jnp.int32, st.shape, 1) + lo_q
                st = jnp.where(r <= c, st, -jnp.inf)
            m_old = m_ref[:, lo_q:]
            m_new = jnp.maximum(m_old, jnp.max(st, axis=0, keepdims=True))
            alpha = jnp.exp2(m_old - m_new)
            p = jnp.exp2(st - m_new)
            m_ref[:, lo_q:] = m_new
            acc_ref[:, lo_q:] = alpha * acc_ref[:, lo_q:] + _dot(vt_ref[j], p.astype(BF16))

        def pairs(n_pairs, first_pair):
            def body(it, _):
                for u in range(n_pairs):
                    j = 2 * (first_pair + n_pairs * it + u)
                    logits(j + 1, s1_ref, 0)
                    consume(j, s0_ref, None, 0)
                    logits(j + 2, s0_ref, 0)
                    consume(j + 1, s1_ref, None, 0)
                return 0
            return body

        n_pairs = i - p0
        logits(2 * p0, s0_ref, 0)
        lax.fori_loop(0, n_pairs // 2, pairs(2, p0), 0)
        lax.fori_loop(0, n_pairs % 2, pairs(1, p0 + 2 * (n_pairs // 2)), 0)
        logits(2 * i + 1, s1_ref, tk)
        consume(2 * i, s0_ref, 0, 0)
        consume(2 * i + 1, s1_ref, 1, tk)
        o_ref[pl.ds(q0, tq), :] = (acc_ref[0:dh, :] / acc_ref[dh:dh + 1, :]).T.astype(o_ref.dtype)
        return 0

    lax.fori_loop(0, S // tq, q_tile, 0)


def _fox_attention(u, cum, *, B, S, H, E, tq=1024, tk=512, ch=512):
    T = u.shape[0]
    dh = FOX_HEAD_DIM
    cum4 = cum.reshape(B, H, S // ch, ch)
    body = functools.partial(_fox_attn_body, S=S, tq=tq, tk=tk, ch=ch, scale=dh ** -0.5)
    return pl.pallas_call(
        body,
        grid=(B, H),
        in_specs=[
            pl.BlockSpec((S, dh), lambda b, h: (b, h)),
            pl.BlockSpec((S, dh), lambda b, h: (b, H + h)),
            pl.BlockSpec((S, dh), lambda b, h: (b, 2 * H + h)),
            pl.BlockSpec((1, 1, S // ch, ch), lambda b, h: (b, h, 0, 0)),
        ],
        out_specs=pl.BlockSpec((S, dh), lambda b, h: (b, h)),
        out_shape=jax.ShapeDtypeStruct((T, E), BF16),
        scratch_shapes=[pltpu.VMEM((S // tk, dh + ONES_ROWS, tk), BF16), pltpu.VMEM((S, 2 * dh), BF16),
                        pltpu.VMEM((S, LANES), F32), pltpu.VMEM((tk, tq), F32), pltpu.VMEM((tk, tq), F32),
                        pltpu.VMEM((tq, 2 * dh), BF16), pltpu.VMEM((1, tq), F32),
                        pltpu.VMEM((dh + ONES_ROWS, tq), F32)],
        compiler_params=_params("parallel", "parallel"),
        name="fox_attention",
    )(u, u, u, cum4)


def _fox_layer(x, g, w_in, b_f, w_out, *, B, S):
    E = w_out.shape[0]
    H = E // FOX_HEAD_DIM
    w_main = w_in[:, :3 * E]
    w_f = w_in[:, 3 * E:3 * E + H]
    w_gate = w_in[:, 3 * E + H:]
    w_cat = jnp.concatenate([w_main, w_gate], axis=1).astype(BF16)
    wf_pad = jnp.pad(w_f, ((0, 0), (0, LANES - H))).astype(BF16)
    bf_pad = jnp.pad(b_f, (0, LANES - H)).reshape(1, LANES)
    u = _rms_matmul(x, g, w_cat)
    cum = _fox_cum(x, g, wf_pad, bf_pad, B=B, S=S, H=H)
    o = _fox_attention(u, cum, B=B, S=S, H=H, E=E)
    return _gated_out(o, u, 3, w_out.astype(BF16), x)


def _gla_gate_body(x_ref, g_ref, w1_ref, w2_ref, b_ref, o_ref):
    hn = _rms(x_ref[...], g_ref[...]).astype(BF16)
    low = _dot(hn, w1_ref[...]).astype(BF16)
    o_ref[...] = _log_sigmoid(_dot(low, w2_ref[...]) + b_ref[...]) * (1.0 / GLA_GATE_NORM)


def _gla_gate(x, g, w1p, w2p, b_g, *, tm=512):
    T, D = x.shape
    KD = w2p.shape[1]
    return pl.pallas_call(
        _gla_gate_body,
        grid=(T // tm,),
        in_specs=[
            pl.BlockSpec((tm, D), lambda i: (i, 0)),
            pl.BlockSpec((1, D), lambda i: (0, 0)),
            pl.BlockSpec((D, LANES), lambda i: (0, 0)),
            pl.BlockSpec((LANES, KD), lambda i: (0, 0)),
            pl.BlockSpec((1, KD), lambda i: (0, 0)),
        ],
        out_specs=pl.BlockSpec((tm, KD), lambda i: (i, 0)),
        out_shape=jax.ShapeDtypeStruct((T, KD), F32),
        compiler_params=_params("parallel"),
        name="gla_log_decay",
    )(x, g, w1p, w2p, b_g)


def _gla_chunk_body(q_ref, k_ref, v_ref, g_ref, gn_ref, o_ref, st_ref, *, tb, dk, dv, scale):
    C = GLA_CHUNK

    @pl.when(pl.program_id(1) == 0)
    def _():
        st_ref[...] = jnp.zeros_like(st_ref)

    r = lax.broadcasted_iota(jnp.int32, (tb, tb), 0)
    c = lax.broadcasted_iota(jnp.int32, (tb, tb), 1)
    shift = C.bit_length() - 1
    same = jnp.right_shift(r, shift) == jnp.right_shift(c, shift)
    tril_blk = jnp.where(jnp.logical_and(same, c <= r), 1.0, 0.0).astype(BF16)
    b_all = _dot_01(tril_blk, g_ref[...])

    rc = lax.broadcasted_iota(jnp.int32, (C, C), 0)
    cc = lax.broadcasted_iota(jnp.int32, (C, C), 1)
    causal = cc <= rc

    for n in range(tb // C):
        rows = slice(n * C, (n + 1) * C)
        for h in range(GLA_HEADS):
            ks_, vs_ = slice(h * dk, (h + 1) * dk), slice(h * dv, (h + 1) * dv)
            qc = q_ref[rows, ks_].astype(F32) * scale
            kc = k_ref[rows, ks_].astype(F32)
            vc = v_ref[rows, vs_]
            b = b_all[rows, ks_]
            b_last = b[C - 1:C, :]
            b_mid = b[C // 2:C // 2 + 1, :]
            st = st_ref[h]
            o_inter = _dot_nt((qc * jnp.exp(b)).astype(BF16), st.astype(BF16))
            qs = (qc * jnp.exp(b - b_mid)).astype(BF16)
            ks = (kc * jnp.exp(b_mid - b)).astype(BF16)
            scores = jnp.where(causal, _dot_nt(qs, ks), 0.0)
            o = o_inter + _dot(scores.astype(BF16), vc)
            k_dec = (kc * jnp.exp(b_last - b)).astype(BF16)
            st_ref[h] = st * jnp.exp(b_last) + _dot_tn(vc, k_dec)
            o_ref[rows, vs_] = _rms(o, gn_ref[...]).astype(o_ref.dtype)


def _gla_chunks(u, gdec, gn_g, *, B, S, E, tb=256):
    T = u.shape[0]
    Hh = GLA_HEADS
    KD = gdec.shape[1]
    dk, dv = KD // Hh, E // Hh
    nS = S // tb
    body = functools.partial(_gla_chunk_body, tb=tb, dk=dk, dv=dv, scale=dk ** -0.5)
    return pl.pallas_call(
        body,
        grid=(B, nS),
        in_specs=[
            pl.BlockSpec((tb, KD), lambda b, i: (b * nS + i, 0)),
            pl.BlockSpec((tb, KD), lambda b, i: (b * nS + i, 1)),
            pl.BlockSpec((tb, E), lambda b, i: (b * nS + i, (2 * KD) // E)),
            pl.BlockSpec((tb, KD), lambda b, i: (b * nS + i, 0)),
            pl.BlockSpec((1, dv), lambda b, i: (0, 0)),
        ],
        out_specs=pl.BlockSpec((tb, E), lambda b, i: (b * nS + i, 0)),
        out_shape=jax.ShapeDtypeStruct((T, E), BF16),
        scratch_shapes=[pltpu.VMEM((Hh, dv, dk), F32)],
        compiler_params=_params("parallel", "arbitrary"),
        name="gla_chunks",
    )(u, u, u, gdec, gn_g)


def _gla_layer(x, g, w_in, w_g1, w_g2, b_g, gn_g, w_out, *, B, S):
    E = w_out.shape[0]
    KD = w_g2.shape[1]
    rank = w_g1.shape[1]
    u = _rms_matmul(x, g, w_in.astype(BF16))
    w1p = jnp.pad(w_g1, ((0, 0), (0, LANES - rank))).astype(BF16)
    w2p = jnp.pad(w_g2, ((0, LANES - rank), (0, 0))).astype(BF16)
    gdec = _gla_gate(x, g, w1p, w2p, b_g.reshape(1, KD))
    o = _gla_chunks(u, gdec, gn_g.reshape(1, -1), B=B, S=S, E=E)
    return _gated_out(o, u, (2 * KD + E) // E, w_out.astype(BF16), x)


def _lru_body(xb_ref, xh_ref, gate_ref, x_ref, wc_ref, bc_ref, wax_ref, ba_ref, bx_ref, lam_ref,
              wout_ref, fg_ref, y_ref, buf_ref, a_ref, b_ref, h_ref, carry_ref, *, tm, E):
    i = pl.program_id(1)

    @pl.when(i == 0)
    def _():
        carry_ref[...] = jnp.zeros_like(carry_ref)

    buf_ref[0:LRU_HALO, :] = jnp.where(i > 0, xh_ref[...].astype(F32), 0.0)
    buf_ref[LRU_HALO:, :] = xb_ref[...].astype(F32)
    off = LRU_HALO - (LRU_CONV_K - 1)
    xc = jnp.broadcast_to(bc_ref[...], (tm, E))
    for k in range(LRU_CONV_K):
        xc = xc + wc_ref[k:k + 1, :] * buf_ref[off + k:off + k + tm, :]

    lam = lam_ref[...]
    sp = jnp.maximum(-lam, 0.0) + jnp.log(1.0 + jnp.exp(-jnp.abs(lam)))
    rate = sp * (-LRU_C * 1.4426950408889634)
    W = LRU_BLOCK_W
    for n in range(E // W):
        sl = slice(n * W, (n + 1) * W)
        xn = xc[:, sl]
        pre = _dot(xn.astype(BF16), wax_ref[n])
        r = _sigmoid(pre[:, :W] + ba_ref[:, sl])
        ig = _sigmoid(pre[:, W:] + bx_ref[:, sl])
        a = jnp.exp2(r * rate[:, sl])
        beta = jnp.sqrt(1.0 - a * a)
        a_ref[:, sl] = a
        b_ref[:, sl] = beta * (ig * xn)

    def step(t, h):
        h = a_ref[pl.ds(t, 1), :] * h + b_ref[pl.ds(t, 1), :]
        h_ref[pl.ds(t, 1), :] = h
        return h

    carry_ref[...] = lax.fori_loop(0, tm, step, carry_ref[...], unroll=8)

    og = (h_ref[...] * _silu(gate_ref[...].astype(F32))).astype(BF16)
    y = x_ref[...] + _dot(og, wout_ref[...])
    y_ref[...] = _rms(y, fg_ref[...])


def _lru_layer(x, g, w_in, w_conv, b_conv, w_a, b_a, w_x, b_x, lam, w_out, final_g, *, B, S, tm=256):
    T, D = x.shape
    E = w_out.shape[0]
    u = _rms_matmul(x, g, w_in.astype(BF16))
    wax = jnp.concatenate([w_a, w_x], axis=-1).astype(BF16)
    nS = S // tm
    hpb = tm // LRU_HALO

    def row(n):
        return pl.BlockSpec((1, n), lambda b, i: (0, 0))

    body = functools.partial(_lru_body, tm=tm, E=E)
    return pl.pallas_call(
        body,
        grid=(B, nS),
        in_specs=[
            pl.BlockSpec((tm, E), lambda b, i: (b * nS + i, 0)),
            pl.BlockSpec((LRU_HALO, E), lambda b, i: (jnp.maximum((b * nS + i) * hpb - 1, 0), 0)),
            pl.BlockSpec((tm, E), lambda b, i: (b * nS + i, 1)),
            pl.BlockSpec((tm, D), lambda b, i: (b * nS + i, 0)),
            pl.BlockSpec((LRU_CONV_K, E), lambda b, i: (0, 0)),
            row(E),
            pl.BlockSpec(wax.shape, lambda b, i: (0, 0, 0)),
            row(E), row(E), row(E),
            pl.BlockSpec((E, D), lambda b, i: (0, 0)),
            row(D),
        ],
        out_specs=pl.BlockSpec((tm, D), lambda b, i: (b * nS + i, 0)),
        out_shape=jax.ShapeDtypeStruct((T, D), F32),
        scratch_shapes=[pltpu.VMEM((tm + LRU_HALO, E), F32), pltpu.VMEM((tm, E), F32),
                        pltpu.VMEM((tm, E), F32), pltpu.VMEM((tm, E), F32), pltpu.VMEM((1, E), F32)],
        compiler_params=_params("parallel", "arbitrary"),
        name="rglru_mixer",
    )(u, u, u, x, w_conv, b_conv.reshape(1, E), wax, b_a.reshape(1, E), b_x.reshape(1, E),
      lam.reshape(1, E), w_out.astype(BF16), final_g.reshape(1, D))


def kernel(x, norm_g, final_g, conv_w_in, conv_w_dw, conv_b_dw, conv_ln_g, conv_ln_b, conv_w_out, fox_w_in, fox_b_f, fox_w_out, gla_w_in, gla_w_g1, gla_w_g2, gla_b_g, gla_gn_g, gla_w_out, lru_w_in, lru_w_conv, lru_b_conv, lru_w_a, lru_b_a, lru_w_x, lru_b_x, lru_lam, lru_w_out):
    B, S, D = x.shape
    assert norm_g.shape[0] == 4 and conv_w_in.shape[0] == 1, "one layer per mixer"
    E = conv_w_out.shape[1]
    h = x.reshape(B * S, D)
    ng = norm_g.reshape(4, 1, D)
    h = _conv_layer(h, ng[0], conv_w_in[0].astype(BF16), conv_w_dw[0], conv_b_dw[0].reshape(1, E),
                    conv_ln_g[0].reshape(1, E), conv_ln_b[0].reshape(1, E), conv_w_out[0].astype(BF16), B=B, S=S)
    h = _fox_layer(h, ng[1], fox_w_in[0], fox_b_f[0], fox_w_out[0], B=B, S=S)
    h = _gla_layer(h, ng[2], gla_w_in[0], gla_w_g1[0], gla_w_g2[0], gla_b_g[0], gla_gn_g[0], gla_w_out[0], B=B, S=S)
    h = _lru_layer(h, ng[3], lru_w_in[0], lru_w_conv[0], lru_b_conv[0], lru_w_a[0], lru_b_a[0], lru_w_x[0],
                   lru_b_x[0], lru_lam[0], lru_w_out[0], final_g, B=B, S=S)
    return h.reshape(B, S, D)
```

```python
import functools

import jax
import jax.numpy as jnp
from jax import lax
from jax.experimental import pallas as pl
from jax.experimental.pallas import tpu as pltpu

F32 = jnp.float32
BF16 = jnp.bfloat16
EPS = 1e-6

V7X_VMEM_BYTES = 64 * 1024 * 1024
VMEM_LIMIT_BYTES = V7X_VMEM_BYTES - 16 * 1024 * 1024
LANES = 128
SUBLANES = 8

CONV_K = 31
CONV_HALO = 32
FOX_HEAD_DIM = 128
ONES_ROWS = 16
UNDERFLOW_BITS = 176.0
NORM_SLACK = 1.01
GLA_HEADS = 4
GLA_CHUNK = 64
GLA_GATE_NORM = 16.0
LRU_CONV_K = 4
LRU_HALO = 16
LRU_BLOCK_W = 128
LRU_C = 8.0


def _params(*sem):
    return pltpu.CompilerParams(dimension_semantics=sem, vmem_limit_bytes=VMEM_LIMIT_BYTES)


def _sigmoid(x):
    return 1.0 / (1.0 + jnp.exp(-x))


def _silu(x):
    return x * _sigmoid(x)


def _log_sigmoid(x):
    return jnp.minimum(x, 0.0) - jnp.log(1.0 + jnp.exp(-jnp.abs(x)))


def _rms(x, g):
    return x * lax.rsqrt(jnp.mean(x * x, axis=-1, keepdims=True) + EPS) * g


def _dot(a, b):
    return jnp.dot(a, b, preferred_element_type=F32)


def _dot_nt(a, b):
    return lax.dot_general(a, b, (((1,), (1,)), ((), ())), preferred_element_type=F32)


def _dot_tn(a, b):
    return lax.dot_general(a, b, (((0,), (0,)), ((), ())), preferred_element_type=F32)


def _dot_01(m01, x):
    x1 = x.astype(BF16)
    r1 = x - x1.astype(F32)
    x2 = r1.astype(BF16)
    x3 = (r1 - x2.astype(F32)).astype(BF16)
    return _dot(m01, x1) + _dot(m01, x2) + _dot(m01, x3)


def _rms_matmul_body(x_ref, g_ref, w_ref, o_ref, hn_ref):
    @pl.when(pl.program_id(1) == 0)
    def _():
        hn_ref[...] = _rms(x_ref[...], g_ref[...]).astype(BF16)

    o_ref[...] = _dot(hn_ref[...], w_ref[...]).astype(o_ref.dtype)


def _rms_matmul(x, g, w, *, tm=1024, tn=2048):
    T, D = x.shape
    N = w.shape[1]
    assert T % tm == 0 and N % tn == 0
    return pl.pallas_call(
        _rms_matmul_body,
        grid=(T // tm, N // tn),
        in_specs=[
            pl.BlockSpec((tm, D), lambda i, j: (i, 0)),
            pl.BlockSpec((1, D), lambda i, j: (0, 0)),
            pl.BlockSpec((D, tn), lambda i, j: (0, j)),
        ],
        out_specs=pl.BlockSpec((tm, tn), lambda i, j: (i, j)),
        out_shape=jax.ShapeDtypeStruct((T, N), BF16),
        scratch_shapes=[pltpu.VMEM((tm, D), BF16)],
        compiler_params=_params("parallel", "arbitrary"),
        name="rms_in_proj",
    )(x, g, w)


def _gated_out_body(o_ref, gate_ref, w_ref, x_ref, y_ref):
    og = (o_ref[...].astype(F32) * _silu(gate_ref[...].astype(F32))).astype(BF16)
    y_ref[...] = x_ref[...] + _dot(og, w_ref[...])


def _gated_out(o, u, gate_blk, w_out, x, *, tm=512):
    T, E = o.shape
    D = x.shape[1]
    assert T % tm == 0
    return pl.pallas_call(
        _gated_out_body,
        grid=(T // tm,),
        in_specs=[
            pl.BlockSpec((tm, E), lambda i: (i, 0)),
            pl.BlockSpec((tm, E), lambda i: (i, gate_blk)),
            pl.BlockSpec((E, D), lambda i: (0, 0)),
            pl.BlockSpec((tm, D), lambda i: (i, 0)),
        ],
        out_specs=pl.BlockSpec((tm, D), lambda i: (i, 0)),
        out_shape=jax.ShapeDtypeStruct((T, D), F32),
        compiler_params=_params("parallel"),
        name="gated_out_proj",
    )(o, u, w_out, x)


def _conv_body(a_ref, b_ref, gate_ref, ah_ref, bh_ref, x_ref, wdw_ref, bdw_ref, lng_ref, lnb_ref,
               wout_ref, y_ref, sh_ref, acc_ref, *, tm, E, rows, cols):
    i = pl.program_id(1)
    off = CONV_HALO - (CONV_K - 1)
    n_sh = tm + CONV_HALO - SUBLANES
    for c0 in range(0, E, cols):
        cs = slice(c0, c0 + cols)
        vh = ah_ref[:, cs].astype(F32) * _sigmoid(bh_ref[:, cs].astype(F32))
        sh_ref[0, 0:CONV_HALO, :] = jnp.where(i > 0, vh, 0.0)
        sh_ref[0, CONV_HALO:, :] = a_ref[:, cs].astype(F32) * _sigmoid(b_ref[:, cs].astype(F32))
        for s in range(1, SUBLANES):
            sh_ref[s, 0:n_sh, :] = sh_ref[0, s:s + n_sh, :]
        for r0 in range(0, tm, rows):
            acc = jnp.broadcast_to(bdw_ref[:, cs], (rows, cols))
            for k in range(CONV_K):
                s, a8 = (off + k) % SUBLANES, (off + k) // SUBLANES * SUBLANES
                wk = jnp.concatenate([wdw_ref[k, :, cs]] * (rows // SUBLANES), axis=0)
                acc = acc + wk * sh_ref[s, r0 + a8:r0 + a8 + rows, :]
            acc_ref[r0:r0 + rows, cs] = acc

    c = acc_ref[...]
    mu = jnp.mean(c, axis=-1, keepdims=True)
    d = c - mu
    var = jnp.mean(d * d, axis=-1, keepdims=True)
    ln = d * lax.rsqrt(var + EPS) * lng_ref[...] + lnb_ref[...]
    og = (_silu(ln) * _silu(gate_ref[...].astype(F32))).astype(BF16)
    y_ref[...] = x_ref[...] + _dot(og, wout_ref[...])


def _conv_layer(x, g, w_in, w_dw, b_dw, ln_g, ln_b, w_out, *, B, S, tm=256):
    T, D = x.shape
    E = w_out.shape[0]
    u = _rms_matmul(x, g, w_in)
    nS = S // tm
    hpb = tm // CONV_HALO

    def tile(col):
        return pl.BlockSpec((tm, E), lambda b, i: (b * nS + i, col))

    def halo(col):
        return pl.BlockSpec((CONV_HALO, E), lambda b, i: (jnp.maximum((b * nS + i) * hpb - 1, 0), col))

    def row(n):
        return pl.BlockSpec((1, n), lambda b, i: (0, 0))

    cols = 512
    body = functools.partial(_conv_body, tm=tm, E=E, rows=32, cols=cols)
    return pl.pallas_call(
        body,
        grid=(B, nS),
        in_specs=[tile(0), tile(1), tile(2), halo(0), halo(1),
                  pl.BlockSpec((tm, D), lambda b, i: (b * nS + i, 0)),
                  pl.BlockSpec((CONV_K, SUBLANES, E), lambda b, i: (0, 0, 0)),
                  row(E), row(E), row(E),
                  pl.BlockSpec((E, D), lambda b, i: (0, 0))],
        out_specs=pl.BlockSpec((tm, D), lambda b, i: (b * nS + i, 0)),
        out_shape=jax.ShapeDtypeStruct((T, D), F32),
        scratch_shapes=[pltpu.VMEM((SUBLANES, tm + CONV_HALO, cols), F32), pltpu.VMEM((tm, E), F32)],
        compiler_params=_params("parallel", "parallel"),
        name="conv_mixer",
    )(u, u, u, u, u, x, jnp.broadcast_to(w_dw[:, None, :], (CONV_K, SUBLANES, E)), b_dw, ln_g, ln_b, w_out)


def _fox_cum_body(x_ref, g_ref, wf_ref, bf_ref, o_ref, carry_ref, *, tc, H):
    @pl.when(pl.program_id(1) == 0)
    def _():
        carry_ref[...] = jnp.zeros_like(carry_ref)

    hn = _rms(x_ref[...], g_ref[...]).astype(BF16)
    log_f = _log_sigmoid(_dot(hn, wf_ref[...]) + bf_ref[...])
    r = lax.broadcasted_iota(jnp.int32, (tc, tc), 0)
    c = lax.broadcasted_iota(jnp.int32, (tc, tc), 1)
    tril = jnp.where(c <= r, 1.0, 0.0).astype(BF16)
    cum = _dot_01(tril, log_f) + carry_ref[...]
    carry_ref[...] = cum[tc - 1:tc, :]
    o_ref[0] = cum.T[0:H, :]


def _fox_cum(x, g, wf, bfp, *, B, S, H, tc=512):
    T, D = x.shape
    nS = S // tc
    body = functools.partial(_fox_cum_body, tc=tc, H=H)
    return pl.pallas_call(
        body,
        grid=(B, nS),
        in_specs=[
            pl.BlockSpec((tc, D), lambda b, i: (b * nS + i, 0)),
            pl.BlockSpec((1, D), lambda b, i: (0, 0)),
            pl.BlockSpec((D, LANES), lambda b, i: (0, 0)),
            pl.BlockSpec((1, LANES), lambda b, i: (0, 0)),
        ],
        out_specs=pl.BlockSpec((1, H, tc), lambda b, i: (b, 0, i)),
        out_shape=jax.ShapeDtypeStruct((B, H, S), F32),
        scratch_shapes=[pltpu.VMEM((1, LANES), F32)],
        compiler_params=_params("parallel", "arbitrary"),
        name="fox_forget_cumsum",
    )(x, g, wf, bfp)


def _fox_attn_body(q_ref, k_ref, v_ref, c_ref, o_ref, vt_ref, ka_ref, cc_ref, s0_ref, s1_ref, qa0_ref, qa1_ref,
                   m_ref, acc_ref, p0_ref, *, S, tq, tk, ch, scale):
    dh = q_ref.shape[1]
    log2e = 1.4426950408889634
    assert tq == 2 * tk, "a query tile spans exactly two kv blocks (pairwise pipelined loop)"
    lane = lax.broadcasted_iota(jnp.int32, (1, LANES), 1)

    def split3(x):
        hi = x.astype(BF16).astype(F32)
        mid = (x - hi).astype(BF16).astype(F32)
        lo = (x - hi - mid).astype(BF16).astype(F32)
        return hi, mid, lo

    ones_sq = jnp.ones((dh, LANES), BF16)

    def max_row_norm2(x):
        rs = _dot((x * x).astype(BF16), ones_sq)
        return jnp.max(rs, axis=0, keepdims=True)

    sel_r = lax.broadcasted_iota(jnp.int32, (SUBLANES, LANES), 0)
    sel_l = lax.broadcasted_iota(jnp.int32, (SUBLANES, LANES), 1)
    sel_sum = jnp.where(sel_r < 3, 1.0, 0.0)
    sel_aug = jnp.where(jnp.logical_and(sel_r < 3, sel_l == sel_r), -1.0,
                        jnp.where(jnp.logical_and(sel_r == 3, jnp.logical_and(sel_l >= 3, sel_l < 6)), 1.0, 0.0))

    kn2 = jnp.zeros((1, LANES), F32)
    for n0 in range(S // ch):
        rows = slice(n0 * ch, (n0 + 1) * ch)
        kn2 = jnp.maximum(kn2, max_row_norm2(k_ref[rows, :].astype(F32)))
        vt = v_ref[rows, :].astype(F32).T.astype(BF16)
        for s0 in range(ch // tk):
            vt_ref[n0 * (ch // tk) + s0, 0:dh, :] = vt[:, s0 * tk:(s0 + 1) * tk]
            vt_ref[n0 * (ch // tk) + s0, dh:, :] = jnp.ones((ONES_ROWS, tk), BF16)
        hi, mid, lo = split3(c_ref[0, 0, n0:n0 + 1, :] * log2e)
        parts = jnp.concatenate([hi, mid, lo, jnp.ones((1, ch), F32), jnp.zeros((SUBLANES - 4, ch), F32)], axis=0)
        cc_ref[rows, :] = _dot_tn(parts, sel_sum)
        ka_ref[rows, 0:dh] = k_ref[rows, :]
        ka_ref[rows, dh:] = _dot_tn(parts, sel_aug).astype(BF16)
    c_end = cc_ref[pl.ds(tk - 1, S // tk, stride=tk), :]

    def scaled_q(i):
        q0 = pl.multiple_of(i * tq, tq)
        return (q_ref[pl.ds(q0, tq), :].astype(F32) * (scale * log2e)).astype(BF16)

    def dead_pairs(i):
        q0 = pl.multiple_of(i * tq, tq)
        qw = scaled_q(i)
        c0 = cc_ref[pl.ds(q0, 1), :]
        qk = jnp.sqrt(max_row_norm2(qw.astype(F32)) * kn2) * NORM_SLACK
        n_dead = jnp.sum(jnp.where(c_end > c0 + (UNDERFLOW_BITS + 2.0 * qk), 1, 0)[:, 0:1])
        return jnp.minimum(n_dead // 2, i)

    def stage(i, qa_ref):
        hi, mid, lo = split3(cc_ref[pl.ds(pl.multiple_of(i * tq, tq), 1), :])
        aug = jnp.where(lane < 3, 1.0, jnp.where(lane == 3, hi, jnp.where(lane == 4, mid,
                        jnp.where(lane == 5, lo, 0.0))))
        qa_ref[:, 0:dh] = scaled_q(i)
        qa_ref[:, dh:] = jnp.broadcast_to(aug, (tq, LANES)).astype(BF16)

    def reset_stats():
        m_ref[...] = jnp.full((1, tq), -jnp.inf, F32)
        acc_ref[...] = jnp.zeros_like(acc_ref)

    def logits(j, s_ref, lo_q, qa_ref):
        kv0 = pl.multiple_of(j * tk, tk)
        s_ref[:, lo_q:] = _dot_nt(ka_ref[pl.ds(kv0, tk), :], qa_ref[lo_q:, :])

    def consume(j, s_ref, diag, lo_q):
        st = s_ref[:, lo_q:]
        if diag is not None:
            r = lax.broadcasted_iota(jnp.int32, st.shape, 0) + diag * tk
            c = lax.broadcasted_iota(jnp.int32, st.shape, 1) + lo_q
            st = jnp.where(r <= c, st, -jnp.inf)
        m_old = m_ref[:, lo_q:]
        m_new = jnp.maximum(m_old, jnp.max(st, axis=0, keepdims=True))
        alpha = jnp.exp2(m_old - m_new)
        p = jnp.exp2(st - m_new)
        m_ref[:, lo_q:] = m_new
        acc_ref[:, lo_q:] = alpha * acc_ref[:, lo_q:] + _dot(vt_ref[j], p.astype(BF16))

    n_tiles = S // tq
    for t in range(n_tiles):
        p0_ref[t] = dead_pairs(t)

    def q_tile(i, qa_ref, qa_next_ref):
        q0 = pl.multiple_of(i * tq, tq)
        p0 = p0_ref[i]

        def pairs(n_pairs, first_pair):
            def body(it, _):
                for u in range(n_pairs):
                    j = 2 * (first_pair + n_pairs * it + u)
                    logits(j + 1, s1_ref, 0, qa_ref)
                    consume(j, s0_ref, None, 0)
                    logits(j + 2, s0_ref, 0, qa_ref)
                    consume(j + 1, s1_ref, None, 0)
                return 0
            return body

        n_pairs = i - p0
        lax.fori_loop(0, n_pairs // 2, pairs(2, p0), 0)
        lax.fori_loop(0, n_pairs % 2, pairs(1, p0 + 2 * (n_pairs // 2)), 0)
        logits(2 * i + 1, s1_ref, tk, qa_ref)
        i_next = jnp.minimum(i + 1, n_tiles - 1)
        stage(i_next, qa_next_ref)
        consume(2 * i, s0_ref, 0, 0)
        logits(2 * p0_ref[i_next], s0_ref, 0, qa_next_ref)
        consume(2 * i + 1, s1_ref, 1, tk)
        o_ref[pl.ds(q0, tq), :] = (acc_ref[0:dh, :] / acc_ref[dh:dh + 1, :]).T.astype(o_ref.dtype)
        reset_stats()

    def tile_pair(it, _):
        q_tile(2 * it, qa0_ref, qa1_ref)
        q_tile(2 * it + 1, qa1_ref, qa0_ref)
        return 0

    assert n_tiles % 2 == 0, "query tiles alternate between two staging buffers"
    stage(0, qa0_ref)
    reset_stats()
    logits(2 * p0_ref[0], s0_ref, 0, qa0_ref)
    lax.fori_loop(0, n_tiles // 2, tile_pair, 0)


def _fox_attention(u, cum, *, B, S, H, E, tq=1024, tk=512, ch=512):
    T = u.shape[0]
    dh = FOX_HEAD_DIM
    cum4 = cum.reshape(B, H, S // ch, ch)
    body = functools.partial(_fox_attn_body, S=S, tq=tq, tk=tk, ch=ch, scale=dh ** -0.5)
    return pl.pallas_call(
        body,
        grid=(B, H),
        in_specs=[
            pl.BlockSpec((S, dh), lambda b, h: (b, h)),
            pl.BlockSpec((S, dh), lambda b, h: (b, H + h)),
            pl.BlockSpec((S, dh), lambda b, h: (b, 2 * H + h)),
            pl.BlockSpec((1, 1, S // ch, ch), lambda b, h: (b, h, 0, 0)),
        ],
        out_specs=pl.BlockSpec((S, dh), lambda b, h: (b, h)),
        out_shape=jax.ShapeDtypeStruct((T, E), BF16),
        scratch_shapes=[pltpu.VMEM((S // tk, dh + ONES_ROWS, tk), BF16), pltpu.VMEM((S, 2 * dh), BF16),
                        pltpu.VMEM((S, LANES), F32), pltpu.VMEM((tk, tq), F32), pltpu.VMEM((tk, tq), F32),
                        pltpu.VMEM((tq, 2 * dh), BF16), pltpu.VMEM((tq, 2 * dh), BF16), pltpu.VMEM((1, tq), F32),
                        pltpu.VMEM((dh + ONES_ROWS, tq), F32), pltpu.SMEM((S // tq,), jnp.int32)],
        compiler_params=_params("parallel", "parallel"),
        name="fox_attention",
    )(u, u, u, cum4)


def _fox_layer(x, g, w_in, b_f, w_out, *, B, S):
    E = w_out.shape[0]
    H = E // FOX_HEAD_DIM
    w_main = w_in[:, :3 * E]
    w_f = w_in[:, 3 * E:3 * E + H]
    w_gate = w_in[:, 3 * E + H:]
    w_cat = jnp.concatenate([w_main, w_gate], axis=1).astype(BF16)
    wf_pad = jnp.pad(w_f, ((0, 0), (0, LANES - H))).astype(BF16)
    bf_pad = jnp.pad(b_f, (0, LANES - H)).reshape(1, LANES)
    u = _rms_matmul(x, g, w_cat)
    cum = _fox_cum(x, g, wf_pad, bf_pad, B=B, S=S, H=H)
    o = _fox_attention(u, cum, B=B, S=S, H=H, E=E)
    return _gated_out(o, u, 3, w_out.astype(BF16), x)


def _gla_gate_body(x_ref, g_ref, w1_ref, w2_ref, b_ref, o_ref):
    hn = _rms(x_ref[...], g_ref[...]).astype(BF16)
    low = _dot(hn, w1_ref[...]).astype(BF16)
    o_ref[...] = _log_sigmoid(_dot(low, w2_ref[...]) + b_ref[...]) * (1.0 / GLA_GATE_NORM)


def _gla_gate(x, g, w1p, w2p, b_g, *, tm=512):
    T, D = x.shape
    KD = w2p.shape[1]
    return pl.pallas_call(
        _gla_gate_body,
        grid=(T // tm,),
        in_specs=[
            pl.BlockSpec((tm, D), lambda i: (i, 0)),
            pl.BlockSpec((1, D), lambda i: (0, 0)),
            pl.BlockSpec((D, LANES), lambda i: (0, 0)),
            pl.BlockSpec((LANES, KD), lambda i: (0, 0)),
            pl.BlockSpec((1, KD), lambda i: (0, 0)),
        ],
        out_specs=pl.BlockSpec((tm, KD), lambda i: (i, 0)),
        out_shape=jax.ShapeDtypeStruct((T, KD), F32),
        compiler_params=_params("parallel"),
        name="gla_log_decay",
    )(x, g, w1p, w2p, b_g)


def _gla_chunk_body(q_ref, k_ref, v_ref, g_ref, gn_ref, o_ref, st_ref, *, tb, dk, dv, scale):
    C = GLA_CHUNK

    @pl.when(pl.program_id(1) == 0)
    def _():
        st_ref[...] = jnp.zeros_like(st_ref)

    r = lax.broadcasted_iota(jnp.int32, (tb, tb), 0)
    c = lax.broadcasted_iota(jnp.int32, (tb, tb), 1)
    shift = C.bit_length() - 1
    same = jnp.right_shift(r, shift) == jnp.right_shift(c, shift)
    tril_blk = jnp.where(jnp.logical_and(same, c <= r), 1.0, 0.0).astype(BF16)
    b_all = _dot_01(tril_blk, g_ref[...])

    rc = lax.broadcasted_iota(jnp.int32, (C, C), 0)
    cc = lax.broadcasted_iota(jnp.int32, (C, C), 1)
    causal = cc <= rc

    for n in range(tb // C):
        rows = slice(n * C, (n + 1) * C)
        for h in range(GLA_HEADS):
            ks_, vs_ = slice(h * dk, (h + 1) * dk), slice(h * dv, (h + 1) * dv)
            qc = q_ref[rows, ks_].astype(F32) * scale
            kc = k_ref[rows, ks_].astype(F32)
            vc = v_ref[rows, vs_]
            b = b_all[rows, ks_]
            b_last = b[C - 1:C, :]
            b_mid = b[C // 2:C // 2 + 1, :]
            st = st_ref[h]
            o_inter = _dot_nt((qc * jnp.exp(b)).astype(BF16), st.astype(BF16))
            qs = (qc * jnp.exp(b - b_mid)).astype(BF16)
            ks = (kc * jnp.exp(b_mid - b)).astype(BF16)
            scores = jnp.where(causal, _dot_nt(qs, ks), 0.0)
            o = o_inter + _dot(scores.astype(BF16), vc)
            k_dec = (kc * jnp.exp(b_last - b)).astype(BF16)
            st_ref[h] = st * jnp.exp(b_last) + _dot_tn(vc, k_dec)
            o_ref[rows, vs_] = _rms(o, gn_ref[...]).astype(o_ref.dtype)


def _gla_chunks(u, gdec, gn_g, *, B, S, E, tb=256):
    T = u.shape[0]
    Hh = GLA_HEADS
    KD = gdec.shape[1]
    dk, dv = KD // Hh, E // Hh
    nS = S // tb
    body = functools.partial(_gla_chunk_body, tb=tb, dk=dk, dv=dv, scale=dk ** -0.5)
    return pl.pallas_call(
        body,
        grid=(B, nS),
        in_specs=[
            pl.BlockSpec((tb, KD), lambda b, i: (b * nS + i, 0)),
            pl.BlockSpec((tb, KD), lambda b, i: (b * nS + i, 1)),
            pl.BlockSpec((tb, E), lambda b, i: (b * nS + i, (2 * KD) // E)),
            pl.BlockSpec((tb, KD), lambda b, i: (b * nS + i, 0)),
            pl.BlockSpec((1, dv), lambda b, i: (0, 0)),
        ],
        out_specs=pl.BlockSpec((tb, E), lambda b, i: (b * nS + i, 0)),
        out_shape=jax.ShapeDtypeStruct((T, E), BF16),
        scratch_shapes=[pltpu.VMEM((Hh, dv, dk), F32)],
        compiler_params=_params("parallel", "arbitrary"),
        name="gla_chunks",
    )(u, u, u, gdec, gn_g)


def _gla_layer(x, g, w_in, w_g1, w_g2, b_g, gn_g, w_out, *, B, S):
    E = w_out.shape[0]
    KD = w_g2.shape[1]
    rank = w_g1.shape[1]
    u = _rms_matmul(x, g, w_in.astype(BF16))
    w1p = jnp.pad(w_g1, ((0, 0), (0, LANES - rank))).astype(BF16)
    w2p = jnp.pad(w_g2, ((0, LANES - rank), (0, 0))).astype(BF16)
    gdec = _gla_gate(x, g, w1p, w2p, b_g.reshape(1, KD))
    o = _gla_chunks(u, gdec, gn_g.reshape(1, -1), B=B, S=S, E=E)
    return _gated_out(o, u, (2 * KD + E) // E, w_out.astype(BF16), x)


def _lru_body(xb_ref, xh_ref, gate_ref, x_ref, wc_ref, bc_ref, wax_ref, ba_ref, bx_ref, lam_ref,
              wout_ref, fg_ref, y_ref, buf_ref, a_ref, b_ref, h_ref, carry_ref, *, tm, E):
    i = pl.program_id(1)

    @pl.when(i == 0)
    def _():
        carry_ref[...] = jnp.zeros_like(carry_ref)

    buf_ref[0:LRU_HALO, :] = jnp.where(i > 0, xh_ref[...].astype(F32), 0.0)
    buf_ref[LRU_HALO:, :] = xb_ref[...].astype(F32)
    off = LRU_HALO - (LRU_CONV_K - 1)
    xc = jnp.broadcast_to(bc_ref[...], (tm, E))
    for k in range(LRU_CONV_K):
        xc = xc + wc_ref[k:k + 1, :] * buf_ref[off + k:off + k + tm, :]

    lam = lam_ref[...]
    sp = jnp.maximum(-lam, 0.0) + jnp.log(1.0 + jnp.exp(-jnp.abs(lam)))
    rate = sp * (-LRU_C * 1.4426950408889634)
    W = LRU_BLOCK_W
    for n in range(E // W):
        sl = slice(n * W, (n + 1) * W)
        xn = xc[:, sl]
        pre = _dot(xn.astype(BF16), wax_ref[n])
        r = _sigmoid(pre[:, :W] + ba_ref[:, sl])
        ig = _sigmoid(pre[:, W:] + bx_ref[:, sl])
        a = jnp.exp2(r * rate[:, sl])
        beta = jnp.sqrt(1.0 - a * a)
        a_ref[:, sl] = a
        b_ref[:, sl] = beta * (ig * xn)

    def step(t, h):
        h = a_ref[pl.ds(t, 1), :] * h + b_ref[pl.ds(t, 1), :]
        h_ref[pl.ds(t, 1), :] = h
        return h

    carry_ref[...] = lax.fori_loop(0, tm, step, carry_ref[...], unroll=8)

    og = (h_ref[...] * _silu(gate_ref[...].astype(F32))).astype(BF16)
    y = x_ref[...] + _dot(og, wout_ref[...])
    y_ref[...] = _rms(y, fg_ref[...])


def _lru_layer(x, g, w_in, w_conv, b_conv, w_a, b_a, w_x, b_x, lam, w_out, final_g, *, B, S, tm=256):
    T, D = x.shape
    E = w_out.shape[0]
    u = _rms_matmul(x, g, w_in.astype(BF16))
    wax = jnp.concatenate([w_a, w_x], axis=-1).astype(BF16)
    nS = S // tm
    hpb = tm // LRU_HALO

    def row(n):
        return pl.BlockSpec((1, n), lambda b, i: (0, 0))

    body = functools.partial(_lru_body, tm=tm, E=E)
    return pl.pallas_call(
        body,
        grid=(B, nS),
        in_specs=[
            pl.BlockSpec((tm, E), lambda b, i: (b * nS + i, 0)),
            pl.BlockSpec((LRU_HALO, E), lambda b, i: (jnp.maximum((b * nS + i) * hpb - 1, 0), 0)),
            pl.BlockSpec((tm, E), lambda b, i: (b * nS + i, 1)),
            pl.BlockSpec((tm, D), lambda b, i: (b * nS + i, 0)),
            pl.BlockSpec((LRU_CONV_K, E), lambda b, i: (0, 0)),
            row(E),
            pl.BlockSpec(wax.shape, lambda b, i: (0, 0, 0)),
            row(E), row(E), row(E),
            pl.BlockSpec((E, D), lambda b, i: (0, 0)),
            row(D),
        ],
        out_specs=pl.BlockSpec((tm, D), lambda b, i: (b * nS + i, 0)),
        out_shape=jax.ShapeDtypeStruct((T, D), F32),
        scratch_shapes=[pltpu.VMEM((tm + LRU_HALO, E), F32), pltpu.VMEM((tm, E), F32),
                        pltpu.VMEM((tm, E), F32), pltpu.VMEM((tm, E), F32), pltpu.VMEM((1, E), F32)],
        compiler_params=_params("parallel", "arbitrary"),
        name="rglru_mixer",
    )(u, u, u, x, w_conv, b_conv.reshape(1, E), wax, b_a.reshape(1, E), b_x.reshape(1, E),
      lam.reshape(1, E), w_out.astype(BF16), final_g.reshape(1, D))


def kernel(x, norm_g, final_g, conv_w_in, conv_w_dw, conv_b_dw, conv_ln_g, conv_ln_b, conv_w_out, fox_w_in, fox_b_f, fox_w_out, gla_w_in, gla_w_g1, gla_w_g2, gla_b_g, gla_gn_g, gla_w_out, lru_w_in, lru_w_conv, lru_b_conv, lru_w_a, lru_b_a, lru_w_x, lru_b_x, lru_lam, lru_w_out):
    B, S, D = x.shape
    assert norm_g.shape[0] == 4 and conv_w_in.shape[0] == 1, "one layer per mixer"
    E = conv_w_out.shape[1]
    h = x.reshape(B * S, D)
    ng = norm_g.reshape(4, 1, D)
    h = _conv_layer(h, ng[0], conv_w_in[0].astype(BF16), conv_w_dw[0], conv_b_dw[0].reshape(1, E),
                    conv_ln_g[0].reshape(1, E), conv_ln_b[0].reshape(1, E), conv_w_out[0].astype(BF16), B=B, S=S)
    h = _fox_layer(h, ng[1], fox_w_in[0], fox_b_f[0], fox_w_out[0], B=B, S=S)
    h = _gla_layer(h, ng[2], gla_w_in[0], gla_w_g1[0], gla_w_g2[0], gla_b_g[0], gla_gn_g[0], gla_w_out[0], B=B, S=S)
    h = _lru_layer(h, ng[3], lru_w_in[0], lru_w_conv[0], lru_b_conv[0], lru_w_a[0], lru_b_a[0], lru_w_x[0],
                   lru_b_x[0], lru_lam[0], lru_w_out[0], final_g, B=B, S=S)
    return h.reshape(B, S, D)
```

```python
import functools

import jax
import jax.numpy as jnp
from jax import lax
from jax.experimental import pallas as pl
from jax.experimental.pallas import tpu as pltpu

F32 = jnp.float32
BF16 = jnp.bfloat16
EPS = 1e-6

V7X_VMEM_BYTES = 64 * 1024 * 1024
VMEM_LIMIT_BYTES = V7X_VMEM_BYTES - 16 * 1024 * 1024
LANES = 128
SUBLANES = 8

CONV_K = 31
CONV_HALO = 32
FOX_HEAD_DIM = 128
ONES_ROWS = 16
UNDERFLOW_BITS = 176.0
NORM_SLACK = 1.01
GLA_HEADS = 4
GLA_CHUNK = 64
GLA_GATE_NORM = 16.0
LRU_CONV_K = 4
LRU_HALO = 16
LRU_BLOCK_W = 128
LRU_C = 8.0


def _params(*sem):
    return pltpu.CompilerParams(dimension_semantics=sem, vmem_limit_bytes=VMEM_LIMIT_BYTES)


def _sigmoid(x):
    return 1.0 / (1.0 + jnp.exp(-x))


def _silu(x):
    return x * _sigmoid(x)


def _log_sigmoid(x):
    return jnp.minimum(x, 0.0) - jnp.log(1.0 + jnp.exp(-jnp.abs(x)))


def _rms(x, g):
    return x * lax.rsqrt(jnp.mean(x * x, axis=-1, keepdims=True) + EPS) * g


def _dot(a, b):
    return jnp.dot(a, b, preferred_element_type=F32)


def _dot_nt(a, b):
    return lax.dot_general(a, b, (((1,), (1,)), ((), ())), preferred_element_type=F32)


def _dot_tn(a, b):
    return lax.dot_general(a, b, (((0,), (0,)), ((), ())), preferred_element_type=F32)


def _dot_01(m01, x):
    x1 = x.astype(BF16)
    r1 = x - x1.astype(F32)
    x2 = r1.astype(BF16)
    x3 = (r1 - x2.astype(F32)).astype(BF16)
    return _dot(m01, x1) + _dot(m01, x2) + _dot(m01, x3)


def _rms_matmul_body(x_ref, g_ref, w_ref, o_ref, hn_ref):
    @pl.when(pl.program_id(1) == 0)
    def _():
        hn_ref[...] = _rms(x_ref[...], g_ref[...]).astype(BF16)

    o_ref[...] = _dot(hn_ref[...], w_ref[...]).astype(o_ref.dtype)


def _rms_matmul(x, g, w, *, tm=1024, tn=2048):
    T, D = x.shape
    N = w.shape[1]
    assert T % tm == 0 and N % tn == 0
    return pl.pallas_call(
        _rms_matmul_body,
        grid=(T // tm, N // tn),
        in_specs=[
            pl.BlockSpec((tm, D), lambda i, j: (i, 0)),
            pl.BlockSpec((1, D), lambda i, j: (0, 0)),
            pl.BlockSpec((D, tn), lambda i, j: (0, j)),
        ],
        out_specs=pl.BlockSpec((tm, tn), lambda i, j: (i, j)),
        out_shape=jax.ShapeDtypeStruct((T, N), BF16),
        scratch_shapes=[pltpu.VMEM((tm, D), BF16)],
        compiler_params=_params("parallel", "arbitrary"),
        name="rms_in_proj",
    )(x, g, w)


def _out_proj_body(og_ref, w_ref, x_ref, y_ref):
    y_ref[...] = x_ref[...] + _dot(og_ref[...], w_ref[...])


def _out_proj(og, w_out, x, *, tm=1024):
    T, E = og.shape
    D = x.shape[1]
    assert T % tm == 0
    return pl.pallas_call(
        _out_proj_body,
        grid=(T // tm,),
        in_specs=[
            pl.BlockSpec((tm, E), lambda i: (i, 0)),
            pl.BlockSpec((E, D), lambda i: (0, 0)),
            pl.BlockSpec((tm, D), lambda i: (i, 0)),
        ],
        out_specs=pl.BlockSpec((tm, D), lambda i: (i, 0)),
        out_shape=jax.ShapeDtypeStruct((T, D), F32),
        compiler_params=_params("parallel"),
        name="out_proj",
    )(og, w_out, x)


def _conv_body(a_ref, b_ref, gate_ref, ah_ref, bh_ref, x_ref, wdw_ref, bdw_ref, lng_ref, lnb_ref,
               wout_ref, y_ref, sh_ref, acc_ref, *, tm, E, rows, cols):
    i = pl.program_id(1)
    off = CONV_HALO - (CONV_K - 1)
    n_sh = tm + CONV_HALO - SUBLANES
    for c0 in range(0, E, cols):
        cs = slice(c0, c0 + cols)
        vh = ah_ref[:, cs].astype(F32) * _sigmoid(bh_ref[:, cs].astype(F32))
        sh_ref[0, 0:CONV_HALO, :] = jnp.where(i > 0, vh, 0.0)
        sh_ref[0, CONV_HALO:, :] = a_ref[:, cs].astype(F32) * _sigmoid(b_ref[:, cs].astype(F32))
        for s in range(1, SUBLANES):
            sh_ref[s, 0:n_sh, :] = sh_ref[0, s:s + n_sh, :]
        for r0 in range(0, tm, rows):
            acc = jnp.broadcast_to(bdw_ref[:, cs], (rows, cols))
            for k in range(CONV_K):
                s, a8 = (off + k) % SUBLANES, (off + k) // SUBLANES * SUBLANES
                wk = jnp.concatenate([wdw_ref[k, :, cs]] * (rows // SUBLANES), axis=0)
                acc = acc + wk * sh_ref[s, r0 + a8:r0 + a8 + rows, :]
            acc_ref[r0:r0 + rows, cs] = acc

    c = acc_ref[...]
    mu = jnp.mean(c, axis=-1, keepdims=True)
    d = c - mu
    var = jnp.mean(d * d, axis=-1, keepdims=True)
    ln = d * lax.rsqrt(var + EPS) * lng_ref[...] + lnb_ref[...]
    og = (_silu(ln) * _silu(gate_ref[...].astype(F32))).astype(BF16)
    y_ref[...] = x_ref[...] + _dot(og, wout_ref[...])


def _conv_layer(x, g, w_in, w_dw, b_dw, ln_g, ln_b, w_out, *, B, S, tm=256):
    T, D = x.shape
    E = w_out.shape[0]
    u = _rms_matmul(x, g, w_in)
    nS = S // tm
    hpb = tm // CONV_HALO

    def tile(col):
        return pl.BlockSpec((tm, E), lambda b, i: (b * nS + i, col))

    def halo(col):
        return pl.BlockSpec((CONV_HALO, E), lambda b, i: (jnp.maximum((b * nS + i) * hpb - 1, 0), col))

    def row(n):
        return pl.BlockSpec((1, n), lambda b, i: (0, 0))

    cols = 512
    body = functools.partial(_conv_body, tm=tm, E=E, rows=32, cols=cols)
    return pl.pallas_call(
        body,
        grid=(B, nS),
        in_specs=[tile(0), tile(1), tile(2), halo(0), halo(1),
                  pl.BlockSpec((tm, D), lambda b, i: (b * nS + i, 0)),
                  pl.BlockSpec((CONV_K, SUBLANES, E), lambda b, i: (0, 0, 0)),
                  row(E), row(E), row(E),
                  pl.BlockSpec((E, D), lambda b, i: (0, 0))],
        out_specs=pl.BlockSpec((tm, D), lambda b, i: (b * nS + i, 0)),
        out_shape=jax.ShapeDtypeStruct((T, D), F32),
        scratch_shapes=[pltpu.VMEM((SUBLANES, tm + CONV_HALO, cols), F32), pltpu.VMEM((tm, E), F32)],
        compiler_params=_params("parallel", "parallel"),
        name="conv_mixer",
    )(u, u, u, u, u, x, jnp.broadcast_to(w_dw[:, None, :], (CONV_K, SUBLANES, E)), b_dw, ln_g, ln_b, w_out)


def _fox_cum_body(x_ref, g_ref, wf_ref, bf_ref, o_ref, carry_ref, *, tc, H):
    @pl.when(pl.program_id(1) == 0)
    def _():
        carry_ref[...] = jnp.zeros_like(carry_ref)

    hn = _rms(x_ref[...], g_ref[...]).astype(BF16)
    log_f = _log_sigmoid(_dot(hn, wf_ref[...]) + bf_ref[...])
    r = lax.broadcasted_iota(jnp.int32, (tc, tc), 0)
    c = lax.broadcasted_iota(jnp.int32, (tc, tc), 1)
    tril = jnp.where(c <= r, 1.0, 0.0).astype(BF16)
    cum = _dot_01(tril, log_f) + carry_ref[...]
    carry_ref[...] = cum[tc - 1:tc, :]
    o_ref[0] = cum.T[0:H, :]


def _fox_cum(x, g, wf, bfp, *, B, S, H, tc=512):
    T, D = x.shape
    nS = S // tc
    body = functools.partial(_fox_cum_body, tc=tc, H=H)
    return pl.pallas_call(
        body,
        grid=(B, nS),
        in_specs=[
            pl.BlockSpec((tc, D), lambda b, i: (b * nS + i, 0)),
            pl.BlockSpec((1, D), lambda b, i: (0, 0)),
            pl.BlockSpec((D, LANES), lambda b, i: (0, 0)),
            pl.BlockSpec((1, LANES), lambda b, i: (0, 0)),
        ],
        out_specs=pl.BlockSpec((1, H, tc), lambda b, i: (b, 0, i)),
        out_shape=jax.ShapeDtypeStruct((B, H, S), F32),
        scratch_shapes=[pltpu.VMEM((1, LANES), F32)],
        compiler_params=_params("parallel", "arbitrary"),
        name="fox_forget_cumsum",
    )(x, g, wf, bfp)


def _fox_attn_body(q_ref, k_ref, v_ref, gate_ref, c_ref, o_ref, vt_ref, ka_ref, cc_ref, s0_ref, s1_ref, qa0_ref, qa1_ref,
                   m_ref, acc_ref, p0_ref, *, S, tq, tk, ch, scale):
    dh = q_ref.shape[1]
    log2e = 1.4426950408889634
    assert tq == 2 * tk, "a query tile spans exactly two kv blocks (pairwise pipelined loop)"
    lane = lax.broadcasted_iota(jnp.int32, (1, LANES), 1)

    def split3(x):
        hi = x.astype(BF16).astype(F32)
        mid = (x - hi).astype(BF16).astype(F32)
        lo = (x - hi - mid).astype(BF16).astype(F32)
        return hi, mid, lo

    ones_sq = jnp.ones((dh, LANES), BF16)

    def max_row_norm2(x):
        rs = _dot((x * x).astype(BF16), ones_sq)
        return jnp.max(rs, axis=0, keepdims=True)

    sel_r = lax.broadcasted_iota(jnp.int32, (SUBLANES, LANES), 0)
    sel_l = lax.broadcasted_iota(jnp.int32, (SUBLANES, LANES), 1)
    sel_sum = jnp.where(sel_r < 3, 1.0, 0.0)
    sel_aug = jnp.where(jnp.logical_and(sel_r < 3, sel_l == sel_r), -1.0,
                        jnp.where(jnp.logical_and(sel_r == 3, jnp.logical_and(sel_l >= 3, sel_l < 6)), 1.0, 0.0))

    kn2 = jnp.zeros((1, LANES), F32)
    for n0 in range(S // ch):
        rows = slice(n0 * ch, (n0 + 1) * ch)
        kn2 = jnp.maximum(kn2, max_row_norm2(k_ref[rows, :].astype(F32)))
        vt = v_ref[rows, :].astype(F32).T.astype(BF16)
        for s0 in range(ch // tk):
            vt_ref[n0 * (ch // tk) + s0, 0:dh, :] = vt[:, s0 * tk:(s0 + 1) * tk]
            vt_ref[n0 * (ch // tk) + s0, dh:, :] = jnp.ones((ONES_ROWS, tk), BF16)
        hi, mid, lo = split3(c_ref[0, 0, n0:n0 + 1, :] * log2e)
        parts = jnp.concatenate([hi, mid, lo, jnp.ones((1, ch), F32), jnp.zeros((SUBLANES - 4, ch), F32)], axis=0)
        cc_ref[rows, :] = _dot_tn(parts, sel_sum)
        ka_ref[rows, 0:dh] = k_ref[rows, :]
        ka_ref[rows, dh:] = _dot_tn(parts, sel_aug).astype(BF16)
    c_end = cc_ref[pl.ds(tk - 1, S // tk, stride=tk), :]

    def scaled_q(i):
        q0 = pl.multiple_of(i * tq, tq)
        return (q_ref[pl.ds(q0, tq), :].astype(F32) * (scale * log2e)).astype(BF16)

    def dead_pairs(i):
        q0 = pl.multiple_of(i * tq, tq)
        qw = scaled_q(i)
        c0 = cc_ref[pl.ds(q0, 1), :]
        qk = jnp.sqrt(max_row_norm2(qw.astype(F32)) * kn2) * NORM_SLACK
        n_dead = jnp.sum(jnp.where(c_end > c0 + (UNDERFLOW_BITS + 2.0 * qk), 1, 0)[:, 0:1])
        return jnp.minimum(n_dead // 2, i)

    def stage(i, qa_ref):
        hi, mid, lo = split3(cc_ref[pl.ds(pl.multiple_of(i * tq, tq), 1), :])
        aug = jnp.where(lane < 3, 1.0, jnp.where(lane == 3, hi, jnp.where(lane == 4, mid,
                        jnp.where(lane == 5, lo, 0.0))))
        qa_ref[:, 0:dh] = scaled_q(i)
        qa_ref[:, dh:] = jnp.broadcast_to(aug, (tq, LANES)).astype(BF16)

    def reset_stats():
        m_ref[...] = jnp.full((1, tq), -jnp.inf, F32)
        acc_ref[...] = jnp.zeros_like(acc_ref)

    def logits(j, s_ref, lo_q, qa_ref):
        kv0 = pl.multiple_of(j * tk, tk)
        s_ref[:, lo_q:] = _dot_nt(ka_ref[pl.ds(kv0, tk), :], qa_ref[lo_q:, :])

    def consume(j, s_ref, diag, lo_q):
        st = s_ref[:, lo_q:]
        if diag is not None:
            r = lax.broadcasted_iota(jnp.int32, st.shape, 0) + diag * tk
            c = lax.broadcasted_iota(jnp.int32, st.shape, 1) + lo_q
            st = jnp.where(r <= c, st, -jnp.inf)
        m_old = m_ref[:, lo_q:]
        m_new = jnp.maximum(m_old, jnp.max(st, axis=0, keepdims=True))
        alpha = jnp.exp2(m_old - m_new)
        p = jnp.exp2(st - m_new)
        m_ref[:, lo_q:] = m_new
        acc_ref[:, lo_q:] = alpha * acc_ref[:, lo_q:] + _dot(vt_ref[j], p.astype(BF16))

    n_tiles = S // tq
    for t in range(n_tiles):
        p0_ref[t] = dead_pairs(t)

    def q_tile(i, qa_ref, qa_next_ref):
        q0 = pl.multiple_of(i * tq, tq)
        p0 = p0_ref[i]

        def pairs(n_pairs, first_pair):
            def body(it, _):
                for u in range(n_pairs):
                    j = 2 * (first_pair + n_pairs * it + u)
                    logits(j + 1, s1_ref, 0, qa_ref)
                    consume(j, s0_ref, None, 0)
                    logits(j + 2, s0_ref, 0, qa_ref)
                    consume(j + 1, s1_ref, None, 0)
                return 0
            return body

        n_pairs = i - p0
        lax.fori_loop(0, n_pairs // 2, pairs(2, p0), 0)
        lax.fori_loop(0, n_pairs % 2, pairs(1, p0 + 2 * (n_pairs // 2)), 0)
        logits(2 * i + 1, s1_ref, tk, qa_ref)
        i_next = jnp.minimum(i + 1, n_tiles - 1)
        stage(i_next, qa_next_ref)
        consume(2 * i, s0_ref, 0, 0)
        logits(2 * p0_ref[i_next], s0_ref, 0, qa_next_ref)
        consume(2 * i + 1, s1_ref, 1, tk)
        o = (acc_ref[0:dh, :] / acc_ref[dh:dh + 1, :]).T
        o_ref[pl.ds(q0, tq), :] = (o * _silu(gate_ref[pl.ds(q0, tq), :].astype(F32))).astype(o_ref.dtype)
        reset_stats()

    def tile_pair(it, _):
        q_tile(2 * it, qa0_ref, qa1_ref)
        q_tile(2 * it + 1, qa1_ref, qa0_ref)
        return 0

    assert n_tiles % 2 == 0, "query tiles alternate between two staging buffers"
    stage(0, qa0_ref)
    reset_stats()
    logits(2 * p0_ref[0], s0_ref, 0, qa0_ref)
    lax.fori_loop(0, n_tiles // 2, tile_pair, 0)


def _fox_attention(u, cum, *, B, S, H, E, tq=1024, tk=512, ch=512):
    T = u.shape[0]
    dh = FOX_HEAD_DIM
    cum4 = cum.reshape(B, H, S // ch, ch)
    body = functools.partial(_fox_attn_body, S=S, tq=tq, tk=tk, ch=ch, scale=dh ** -0.5)
    return pl.pallas_call(
        body,
        grid=(B, H),
        in_specs=[
            pl.BlockSpec((S, dh), lambda b, h: (b, h)),
            pl.BlockSpec((S, dh), lambda b, h: (b, H + h)),
            pl.BlockSpec((S, dh), lambda b, h: (b, 2 * H + h)),
            pl.BlockSpec((S, dh), lambda b, h: (b, 3 * H + h)),
            pl.BlockSpec((1, 1, S // ch, ch), lambda b, h: (b, h, 0, 0)),
        ],
        out_specs=pl.BlockSpec((S, dh), lambda b, h: (b, h)),
        out_shape=jax.ShapeDtypeStruct((T, E), BF16),
        scratch_shapes=[pltpu.VMEM((S // tk, dh + ONES_ROWS, tk), BF16), pltpu.VMEM((S, 2 * dh), BF16),
                        pltpu.VMEM((S, LANES), F32), pltpu.VMEM((tk, tq), F32), pltpu.VMEM((tk, tq), F32),
                        pltpu.VMEM((tq, 2 * dh), BF16), pltpu.VMEM((tq, 2 * dh), BF16), pltpu.VMEM((1, tq), F32),
                        pltpu.VMEM((dh + ONES_ROWS, tq), F32), pltpu.SMEM((S // tq,), jnp.int32)],
        compiler_params=_params("parallel", "parallel"),
        name="fox_attention",
    )(u, u, u, u, cum4)


def _fox_layer(x, g, w_in, b_f, w_out, *, B, S):
    E = w_out.shape[0]
    H = E // FOX_HEAD_DIM
    w_main = w_in[:, :3 * E]
    w_f = w_in[:, 3 * E:3 * E + H]
    w_gate = w_in[:, 3 * E + H:]
    w_cat = jnp.concatenate([w_main, w_gate], axis=1).astype(BF16)
    wf_pad = jnp.pad(w_f, ((0, 0), (0, LANES - H))).astype(BF16)
    bf_pad = jnp.pad(b_f, (0, LANES - H)).reshape(1, LANES)
    u = _rms_matmul(x, g, w_cat)
    cum = _fox_cum(x, g, wf_pad, bf_pad, B=B, S=S, H=H)
    o = _fox_attention(u, cum, B=B, S=S, H=H, E=E)
    return _out_proj(o, w_out.astype(BF16), x)


def _gla_in_proj_body(x_ref, g_ref, w_ref, w1_ref, w2_ref, b_ref, o_ref, d_ref, hn_ref, low_ref):
    @pl.when(pl.program_id(1) == 0)
    def _():
        hn = _rms(x_ref[...], g_ref[...]).astype(BF16)
        hn_ref[...] = hn
        low_ref[...] = _dot(hn, w1_ref[...]).astype(BF16)

    o_ref[...] = _dot(hn_ref[...], w_ref[...]).astype(o_ref.dtype)
    d_ref[...] = _log_sigmoid(_dot(low_ref[...], w2_ref[...]) + b_ref[...]) * (1.0 / GLA_GATE_NORM)


def _gla_in_proj(x, g, w, w1p, w2p, b_g, *, tm=1024, nj=2):
    T, D = x.shape
    N, KD = w.shape[1], w2p.shape[1]
    tn, td = N // nj, KD // nj
    assert T % tm == 0 and tn % LANES == 0 and td % LANES == 0
    return pl.pallas_call(
        _gla_in_proj_body,
        grid=(T // tm, nj),
        in_specs=[
            pl.BlockSpec((tm, D), lambda i, j: (i, 0)),
            pl.BlockSpec((1, D), lambda i, j: (0, 0)),
            pl.BlockSpec((D, tn), lambda i, j: (0, j)),
            pl.BlockSpec((D, LANES), lambda i, j: (0, 0)),
            pl.BlockSpec((LANES, td), lambda i, j: (0, j)),
            pl.BlockSpec((1, td), lambda i, j: (0, j)),
        ],
        out_specs=[pl.BlockSpec((tm, tn), lambda i, j: (i, j)), pl.BlockSpec((tm, td), lambda i, j: (i, j))],
        out_shape=[jax.ShapeDtypeStruct((T, N), BF16), jax.ShapeDtypeStruct((T, KD), F32)],
        scratch_shapes=[pltpu.VMEM((tm, D), BF16), pltpu.VMEM((tm, LANES), BF16)],
        compiler_params=_params("parallel", "arbitrary"),
        name="gla_in_proj",
    )(x, g, w, w1p, w2p, b_g)


def _gla_chunk_body(q_ref, k_ref, v_ref, gate_ref, g_ref, gn_ref, o_ref, st_ref, *, tb, dk, dv, scale):
    C = GLA_CHUNK

    @pl.when(pl.program_id(1) == 0)
    def _():
        st_ref[...] = jnp.zeros_like(st_ref)

    r = lax.broadcasted_iota(jnp.int32, (tb, tb), 0)
    c = lax.broadcasted_iota(jnp.int32, (tb, tb), 1)
    shift = C.bit_length() - 1
    same = jnp.right_shift(r, shift) == jnp.right_shift(c, shift)
    tril_blk = jnp.where(jnp.logical_and(same, c <= r), 1.0, 0.0).astype(BF16)
    b_all = _dot_01(tril_blk, g_ref[...])

    rc = lax.broadcasted_iota(jnp.int32, (C, C), 0)
    cc = lax.broadcasted_iota(jnp.int32, (C, C), 1)
    causal = cc <= rc

    for n in range(tb // C):
        rows = slice(n * C, (n + 1) * C)
        for h in range(GLA_HEADS):
            ks_, vs_ = slice(h * dk, (h + 1) * dk), slice(h * dv, (h + 1) * dv)
            qc = q_ref[rows, ks_].astype(F32) * scale
            kc = k_ref[rows, ks_].astype(F32)
            vc = v_ref[rows, vs_]
            b = b_all[rows, ks_]
            b_last = b[C - 1:C, :]
            b_mid = b[C // 2:C // 2 + 1, :]
            st = st_ref[h]
            o_inter = _dot_nt((qc * jnp.exp(b)).astype(BF16), st.astype(BF16))
            qs = (qc * jnp.exp(b - b_mid)).astype(BF16)
            ks = (kc * jnp.exp(b_mid - b)).astype(BF16)
            scores = jnp.where(causal, _dot_nt(qs, ks), 0.0)
            o = o_inter + _dot(scores.astype(BF16), vc)
            k_dec = (kc * jnp.exp(b_last - b)).astype(BF16)
            st_ref[h] = st * jnp.exp(b_last) + _dot_tn(vc, k_dec)
            gated = _rms(o, gn_ref[...]) * _silu(gate_ref[rows, vs_].astype(F32))
            o_ref[rows, vs_] = gated.astype(o_ref.dtype)


def _gla_chunks(u, gdec, gn_g, *, B, S, E, tb=256):
    T = u.shape[0]
    Hh = GLA_HEADS
    KD = gdec.shape[1]
    dk, dv = KD // Hh, E // Hh
    nS = S // tb
    body = functools.partial(_gla_chunk_body, tb=tb, dk=dk, dv=dv, scale=dk ** -0.5)
    return pl.pallas_call(
        body,
        grid=(B, nS),
        in_specs=[
            pl.BlockSpec((tb, KD), lambda b, i: (b * nS + i, 0)),
            pl.BlockSpec((tb, KD), lambda b, i: (b * nS + i, 1)),
            pl.BlockSpec((tb, E), lambda b, i: (b * nS + i, (2 * KD) // E)),
            pl.BlockSpec((tb, E), lambda b, i: (b * nS + i, (2 * KD + E) // E)),
            pl.BlockSpec((tb, KD), lambda b, i: (b * nS + i, 0)),
            pl.BlockSpec((1, dv), lambda b, i: (0, 0)),
        ],
        out_specs=pl.BlockSpec((tb, E), lambda b, i: (b * nS + i, 0)),
        out_shape=jax.ShapeDtypeStruct((T, E), BF16),
        scratch_shapes=[pltpu.VMEM((Hh, dv, dk), F32)],
        compiler_params=_params("parallel", "arbitrary"),
        name="gla_chunks",
    )(u, u, u, u, gdec, gn_g)


def _gla_layer(x, g, w_in, w_g1, w_g2, b_g, gn_g, w_out, *, B, S):
    E = w_out.shape[0]
    KD = w_g2.shape[1]
    rank = w_g1.shape[1]
    w1p = jnp.pad(w_g1, ((0, 0), (0, LANES - rank))).astype(BF16)
    w2p = jnp.pad(w_g2, ((0, LANES - rank), (0, 0))).astype(BF16)
    u, gdec = _gla_in_proj(x, g, w_in.astype(BF16), w1p, w2p, b_g.reshape(1, KD))
    o = _gla_chunks(u, gdec, gn_g.reshape(1, -1), B=B, S=S, E=E)
    return _out_proj(o, w_out.astype(BF16), x)


def _lru_body(xb_ref, xh_ref, gate_ref, x_ref, wc_ref, bc_ref, wax_ref, ba_ref, bx_ref, lam_ref,
              wout_ref, fg_ref, y_ref, buf_ref, a_ref, b_ref, h_ref, carry_ref, *, tm, E):
    i = pl.program_id(1)

    @pl.when(i == 0)
    def _():
        carry_ref[...] = jnp.zeros_like(carry_ref)

    buf_ref[0:LRU_HALO, :] = jnp.where(i > 0, xh_ref[...].astype(F32), 0.0)
    buf_ref[LRU_HALO:, :] = xb_ref[...].astype(F32)
    off = LRU_HALO - (LRU_CONV_K - 1)
    xc = jnp.broadcast_to(bc_ref[...], (tm, E))
    for k in range(LRU_CONV_K):
        xc = xc + wc_ref[k:k + 1, :] * buf_ref[off + k:off + k + tm, :]

    lam = lam_ref[...]
    sp = jnp.maximum(-lam, 0.0) + jnp.log(1.0 + jnp.exp(-jnp.abs(lam)))
    rate = sp * (-LRU_C * 1.4426950408889634)
    W = LRU_BLOCK_W
    for n in range(E // W):
        sl = slice(n * W, (n + 1) * W)
        xn = xc[:, sl]
        pre = _dot(xn.astype(BF16), wax_ref[n])
        r = _sigmoid(pre[:, :W] + ba_ref[:, sl])
        ig = _sigmoid(pre[:, W:] + bx_ref[:, sl])
        a = jnp.exp2(r * rate[:, sl])
        beta = jnp.sqrt(1.0 - a * a)
        a_ref[:, sl] = a
        b_ref[:, sl] = beta * (ig * xn)

    def step(t, h):
        h = a_ref[pl.ds(t, 1), :] * h + b_ref[pl.ds(t, 1), :]
        h_ref[pl.ds(t, 1), :] = h
        return h

    carry_ref[...] = lax.fori_loop(0, tm, step, carry_ref[...], unroll=8)

    og = (h_ref[...] * _silu(gate_ref[...].astype(F32))).astype(BF16)
    y = x_ref[...] + _dot(og, wout_ref[...])
    y_ref[...] = _rms(y, fg_ref[...])


def _lru_layer(x, g, w_in, w_conv, b_conv, w_a, b_a, w_x, b_x, lam, w_out, final_g, *, B, S, tm=256):
    T, D = x.shape
    E = w_out.shape[0]
    u = _rms_matmul(x, g, w_in.astype(BF16))
    wax = jnp.concatenate([w_a, w_x], axis=-1).astype(BF16)
    nS = S // tm
    hpb = tm // LRU_HALO

    def row(n):
        return pl.BlockSpec((1, n), lambda b, i: (0, 0))

    body = functools.partial(_lru_body, tm=tm, E=E)
    return pl.pallas_call(
        body,
        grid=(B, nS),
        in_specs=[
            pl.BlockSpec((tm, E), lambda b, i: (b * nS + i, 0)),
            pl.BlockSpec((LRU_HALO, E), lambda b, i: (jnp.maximum((b * nS + i) * hpb - 1, 0), 0)),
            pl.BlockSpec((tm, E), lambda b, i: (b * nS + i, 1)),
            pl.BlockSpec((tm, D), lambda b, i: (b * nS + i, 0)),
            pl.BlockSpec((LRU_CONV_K, E), lambda b, i: (0, 0)),
            row(E),
            pl.BlockSpec(wax.shape, lambda b, i: (0, 0, 0)),
            row(E), row(E), row(E),
            pl.BlockSpec((E, D), lambda b, i: (0, 0)),
            row(D),
        ],
        out_specs=pl.BlockSpec((tm, D), lambda b, i: (b * nS + i, 0)),
        out_shape=jax.ShapeDtypeStruct((T, D), F32),
        scratch_shapes=[pltpu.VMEM((tm + LRU_HALO, E), F32), pltpu.VMEM((tm, E), F32),
                        pltpu.VMEM((tm, E), F32), pltpu.VMEM((tm, E), F32), pltpu.VMEM((1, E), F32)],
        compiler_params=_params("parallel", "arbitrary"),
        name="rglru_mixer",
    )(u, u, u, x, w_conv, b_conv.reshape(1, E), wax, b_a.reshape(1, E), b_x.reshape(1, E),
      lam.reshape(1, E), w_out.astype(BF16), final_g.reshape(1, D))


def kernel(x, norm_g, final_g, conv_w_in, conv_w_dw, conv_b_dw, conv_ln_g, conv_ln_b, conv_w_out, fox_w_in, fox_b_f, fox_w_out, gla_w_in, gla_w_g1, gla_w_g2, gla_b_g, gla_gn_g, gla_w_out, lru_w_in, lru_w_conv, lru_b_conv, lru_w_a, lru_b_a, lru_w_x, lru_b_x, lru_lam, lru_w_out):
    B, S, D = x.shape
    assert norm_g.shape[0] == 4 and conv_w_in.shape[0] == 1, "one layer per mixer"
    E = conv_w_out.shape[1]
    h = x.reshape(B * S, D)
    ng = norm_g.reshape(4, 1, D)
    h = _conv_layer(h, ng[0], conv_w_in[0].astype(BF16), conv_w_dw[0], conv_b_dw[0].reshape(1, E),
                    conv_ln_g[0].reshape(1, E), conv_ln_b[0].reshape(1, E), conv_w_out[0].astype(BF16), B=B, S=S)
    h = _fox_layer(h, ng[1], fox_w_in[0], fox_b_f[0], fox_w_out[0], B=B, S=S)
    h = _gla_layer(h, ng[2], gla_w_in[0], gla_w_g1[0], gla_w_g2[0], gla_b_g[0], gla_gn_g[0], gla_w_out[0], B=B, S=S)
    h = _lru_layer(h, ng[3], lru_w_in[0], lru_w_conv[0], lru_b_conv[0], lru_w_a[0], lru_b_a[0], lru_w_x[0],
                   lru_b_x[0], lru_lam[0], lru_w_out[0], final_g, B=B, S=S)
    return h.reshape(B, S, D)
```

```python
import functools

import jax
import jax.numpy as jnp
from jax import lax
from jax.experimental import pallas as pl
from jax.experimental.pallas import tpu as pltpu

F32 = jnp.float32
BF16 = jnp.bfloat16
EPS = 1e-6

V7X_VMEM_BYTES = 64 * 1024 * 1024
VMEM_LIMIT_BYTES = V7X_VMEM_BYTES - 16 * 1024 * 1024
LANES = 128
SUBLANES = 8

CONV_K = 31
CONV_HALO = 32
FOX_HEAD_DIM = 128
ONES_ROWS = 16
UNDERFLOW_BITS = 176.0
NORM_SLACK = 1.01
GLA_HEADS = 4
GLA_CHUNK = 64
GLA_GATE_NORM = 16.0
LRU_CONV_K = 4
LRU_HALO = 16
LRU_BLOCK_W = 128
LRU_C = 8.0


def _params(*sem):
    return pltpu.CompilerParams(dimension_semantics=sem, vmem_limit_bytes=VMEM_LIMIT_BYTES)


def _sigmoid(x):
    return 1.0 / (1.0 + jnp.exp(-x))


def _silu(x):
    return x * _sigmoid(x)


def _log_sigmoid(x):
    return jnp.minimum(x, 0.0) - jnp.log(1.0 + jnp.exp(-jnp.abs(x)))


def _rms(x, g):
    return x * lax.rsqrt(jnp.mean(x * x, axis=-1, keepdims=True) + EPS) * g


def _dot(a, b):
    return jnp.dot(a, b, preferred_element_type=F32)


def _dot_nt(a, b):
    return lax.dot_general(a, b, (((1,), (1,)), ((), ())), preferred_element_type=F32)


def _dot_tn(a, b):
    return lax.dot_general(a, b, (((0,), (0,)), ((), ())), preferred_element_type=F32)


def _dot_01(m01, x):
    x1 = x.astype(BF16)
    r1 = x - x1.astype(F32)
    x2 = r1.astype(BF16)
    x3 = (r1 - x2.astype(F32)).astype(BF16)
    return _dot(m01, x1) + _dot(m01, x2) + _dot(m01, x3)


def _rms_matmul_body(x_ref, g_ref, w_ref, o_ref, hn_ref):
    @pl.when(pl.program_id(1) == 0)
    def _():
        hn_ref[...] = _rms(x_ref[...], g_ref[...]).astype(BF16)

    o_ref[...] = _dot(hn_ref[...], w_ref[...]).astype(o_ref.dtype)


def _rms_matmul(x, g, w, *, tm=1024, tn=2048):
    T, D = x.shape
    N = w.shape[1]
    assert T % tm == 0 and N % tn == 0
    return pl.pallas_call(
        _rms_matmul_body,
        grid=(T // tm, N // tn),
        in_specs=[
            pl.BlockSpec((tm, D), lambda i, j: (i, 0)),
            pl.BlockSpec((1, D), lambda i, j: (0, 0)),
            pl.BlockSpec((D, tn), lambda i, j: (0, j)),
        ],
        out_specs=pl.BlockSpec((tm, tn), lambda i, j: (i, j)),
        out_shape=jax.ShapeDtypeStruct((T, N), BF16),
        scratch_shapes=[pltpu.VMEM((tm, D), BF16)],
        compiler_params=_params("parallel", "arbitrary"),
        name="rms_in_proj",
    )(x, g, w)


def _out_proj_body(og_ref, w_ref, x_ref, y_ref):
    y_ref[...] = x_ref[...] + _dot(og_ref[...], w_ref[...])


def _out_proj(og, w_out, x, *, tm=1024):
    T, E = og.shape
    D = x.shape[1]
    assert T % tm == 0
    return pl.pallas_call(
        _out_proj_body,
        grid=(T // tm,),
        in_specs=[
            pl.BlockSpec((tm, E), lambda i: (i, 0)),
            pl.BlockSpec((E, D), lambda i: (0, 0)),
            pl.BlockSpec((tm, D), lambda i: (i, 0)),
        ],
        out_specs=pl.BlockSpec((tm, D), lambda i: (i, 0)),
        out_shape=jax.ShapeDtypeStruct((T, D), F32),
        compiler_params=_params("parallel"),
        name="out_proj",
    )(og, w_out, x)


def _conv_body(a_ref, b_ref, gate_ref, ah_ref, bh_ref, x_ref, wdw_ref, bdw_ref, lng_ref, lnb_ref,
               wout_ref, y_ref, sh_ref, acc_ref, *, tm, E, rows, cols):
    i = pl.program_id(1)
    off = CONV_HALO - (CONV_K - 1)
    n_sh = tm + CONV_HALO - SUBLANES
    for c0 in range(0, E, cols):
        cs = slice(c0, c0 + cols)
        vh = ah_ref[:, cs].astype(F32) * _sigmoid(bh_ref[:, cs].astype(F32))
        sh_ref[0, 0:CONV_HALO, :] = jnp.where(i > 0, vh, 0.0)
        sh_ref[0, CONV_HALO:, :] = a_ref[:, cs].astype(F32) * _sigmoid(b_ref[:, cs].astype(F32))
        for s in range(1, SUBLANES):
            sh_ref[s, 0:n_sh, :] = sh_ref[0, s:s + n_sh, :]
        for r0 in range(0, tm, rows):
            acc = jnp.broadcast_to(bdw_ref[:, cs], (rows, cols))
            for k in range(CONV_K):
                s, a8 = (off + k) % SUBLANES, (off + k) // SUBLANES * SUBLANES
                wk = jnp.concatenate([wdw_ref[k, :, cs]] * (rows // SUBLANES), axis=0)
                acc = acc + wk * sh_ref[s, r0 + a8:r0 + a8 + rows, :]
            acc_ref[r0:r0 + rows, cs] = acc

    c = acc_ref[...]
    mu = jnp.mean(c, axis=-1, keepdims=True)
    d = c - mu
    var = jnp.mean(d * d, axis=-1, keepdims=True)
    ln = d * lax.rsqrt(var + EPS) * lng_ref[...] + lnb_ref[...]
    og = (_silu(ln) * _silu(gate_ref[...].astype(F32))).astype(BF16)
    y_ref[...] = x_ref[...] + _dot(og, wout_ref[...])


def _conv_layer(x, g, w_in, w_dw, b_dw, ln_g, ln_b, w_out, *, B, S, tm=512):
    T, D = x.shape
    E = w_out.shape[0]
    u = _rms_matmul(x, g, w_in)
    nS = S // tm
    hpb = tm // CONV_HALO

    def tile(col):
        return pl.BlockSpec((tm, E), lambda b, i: (b * nS + i, col))

    def halo(col):
        return pl.BlockSpec((CONV_HALO, E), lambda b, i: (jnp.maximum((b * nS + i) * hpb - 1, 0), col))

    def row(n):
        return pl.BlockSpec((1, n), lambda b, i: (0, 0))

    cols = 512
    body = functools.partial(_conv_body, tm=tm, E=E, rows=32, cols=cols)
    return pl.pallas_call(
        body,
        grid=(B, nS),
        in_specs=[tile(0), tile(1), tile(2), halo(0), halo(1),
                  pl.BlockSpec((tm, D), lambda b, i: (b * nS + i, 0)),
                  pl.BlockSpec((CONV_K, SUBLANES, E), lambda b, i: (0, 0, 0)),
                  row(E), row(E), row(E),
                  pl.BlockSpec((E, D), lambda b, i: (0, 0))],
        out_specs=pl.BlockSpec((tm, D), lambda b, i: (b * nS + i, 0)),
        out_shape=jax.ShapeDtypeStruct((T, D), F32),
        scratch_shapes=[pltpu.VMEM((SUBLANES, tm + CONV_HALO, cols), F32), pltpu.VMEM((tm, E), F32)],
        compiler_params=_params("parallel", "parallel"),
        name="conv_mixer",
    )(u, u, u, u, u, x, jnp.broadcast_to(w_dw[:, None, :], (CONV_K, SUBLANES, E)), b_dw, ln_g, ln_b, w_out)


def _fox_cum_body(x_ref, g_ref, wf_ref, bf_ref, o_ref, carry_ref, *, tc, H):
    @pl.when(pl.program_id(1) == 0)
    def _():
        carry_ref[...] = jnp.zeros_like(carry_ref)

    hn = _rms(x_ref[...], g_ref[...]).astype(BF16)
    log_f = _log_sigmoid(_dot(hn, wf_ref[...]) + bf_ref[...])
    r = lax.broadcasted_iota(jnp.int32, (tc, tc), 0)
    c = lax.broadcasted_iota(jnp.int32, (tc, tc), 1)
    tril = jnp.where(c <= r, 1.0, 0.0).astype(BF16)
    cum = _dot_01(tril, log_f) + carry_ref[...]
    carry_ref[...] = cum[tc - 1:tc, :]
    o_ref[0] = cum.T[0:H, :]


def _fox_cum(x, g, wf, bfp, *, B, S, H, tc=512):
    T, D = x.shape
    nS = S // tc
    body = functools.partial(_fox_cum_body, tc=tc, H=H)
    return pl.pallas_call(
        body,
        grid=(B, nS),
        in_specs=[
            pl.BlockSpec((tc, D), lambda b, i: (b * nS + i, 0)),
            pl.BlockSpec((1, D), lambda b, i: (0, 0)),
            pl.BlockSpec((D, LANES), lambda b, i: (0, 0)),
            pl.BlockSpec((1, LANES), lambda b, i: (0, 0)),
        ],
        out_specs=pl.BlockSpec((1, H, tc), lambda b, i: (b, 0, i)),
        out_shape=jax.ShapeDtypeStruct((B, H, S), F32),
        scratch_shapes=[pltpu.VMEM((1, LANES), F32)],
        compiler_params=_params("parallel", "arbitrary"),
        name="fox_forget_cumsum",
    )(x, g, wf, bfp)


def _fox_attn_body(q_ref, k_ref, v_ref, gate_ref, c_ref, o_ref, vt_ref, ka_ref, cc_ref, s0_ref, s1_ref, qa0_ref, qa1_ref,
                   m_ref, acc_ref, p0_ref, *, S, tq, tk, ch, scale):
    dh = q_ref.shape[1]
    log2e = 1.4426950408889634
    assert tq == 2 * tk, "a query tile spans exactly two kv blocks (pairwise pipelined loop)"
    lane = lax.broadcasted_iota(jnp.int32, (1, LANES), 1)

    def split3(x):
        hi = x.astype(BF16).astype(F32)
        mid = (x - hi).astype(BF16).astype(F32)
        lo = (x - hi - mid).astype(BF16).astype(F32)
        return hi, mid, lo

    ones_sq = jnp.ones((dh, LANES), BF16)

    def max_row_norm2(x):
        rs = _dot((x * x).astype(BF16), ones_sq)
        return jnp.max(rs, axis=0, keepdims=True)

    sel_r = lax.broadcasted_iota(jnp.int32, (SUBLANES, LANES), 0)
    sel_l = lax.broadcasted_iota(jnp.int32, (SUBLANES, LANES), 1)
    sel_sum = jnp.where(sel_r < 3, 1.0, 0.0)
    sel_aug = jnp.where(jnp.logical_and(sel_r < 3, sel_l == sel_r), -1.0,
                        jnp.where(jnp.logical_and(sel_r == 3, jnp.logical_and(sel_l >= 3, sel_l < 6)), 1.0, 0.0))

    kn2 = jnp.zeros((1, LANES), F32)
    for n0 in range(S // ch):
        rows = slice(n0 * ch, (n0 + 1) * ch)
        kn2 = jnp.maximum(kn2, max_row_norm2(k_ref[rows, :].astype(F32)))
        vt = v_ref[rows, :].astype(F32).T.astype(BF16)
        for s0 in range(ch // tk):
            vt_ref[n0 * (ch // tk) + s0, 0:dh, :] = vt[:, s0 * tk:(s0 + 1) * tk]
            vt_ref[n0 * (ch // tk) + s0, dh:, :] = jnp.ones((ONES_ROWS, tk), BF16)
        hi, mid, lo = split3(c_ref[0, 0, n0:n0 + 1, :] * log2e)
        parts = jnp.concatenate([hi, mid, lo, jnp.ones((1, ch), F32), jnp.zeros((SUBLANES - 4, ch), F32)], axis=0)
        cc_ref[rows, :] = _dot_tn(parts, sel_sum)
        ka_ref[rows, 0:dh] = k_ref[rows, :]
        ka_ref[rows, dh:] = _dot_tn(parts, sel_aug).astype(BF16)
    c_end = cc_ref[pl.ds(tk - 1, S // tk, stride=tk), :]

    def scaled_q(i):
        q0 = pl.multiple_of(i * tq, tq)
        return (q_ref[pl.ds(q0, tq), :].astype(F32) * (scale * log2e)).astype(BF16)

    def dead_pairs(i):
        q0 = pl.multiple_of(i * tq, tq)
        qw = scaled_q(i)
        c0 = cc_ref[pl.ds(q0, 1), :]
        qk = jnp.sqrt(max_row_norm2(qw.astype(F32)) * kn2) * NORM_SLACK
        n_dead = jnp.sum(jnp.where(c_end > c0 + (UNDERFLOW_BITS + 2.0 * qk), 1, 0)[:, 0:1])
        return jnp.minimum(n_dead // 2, i)

    def stage(i, qa_ref):
        hi, mid, lo = split3(cc_ref[pl.ds(pl.multiple_of(i * tq, tq), 1), :])
        aug = jnp.where(lane < 3, 1.0, jnp.where(lane == 3, hi, jnp.where(lane == 4, mid,
                        jnp.where(lane == 5, lo, 0.0))))
        qa_ref[:, 0:dh] = scaled_q(i)
        qa_ref[:, dh:] = jnp.broadcast_to(aug, (tq, LANES)).astype(BF16)

    def reset_stats():
        m_ref[...] = jnp.full((1, tq), -jnp.inf, F32)
        acc_ref[...] = jnp.zeros_like(acc_ref)

    def logits(j, s_ref, lo_q, qa_ref):
        kv0 = pl.multiple_of(j * tk, tk)
        s_ref[:, lo_q:] = _dot_nt(ka_ref[pl.ds(kv0, tk), :], qa_ref[lo_q:, :])

    def consume(j, s_ref, diag, lo_q):
        st = s_ref[:, lo_q:]
        if diag is not None:
            r = lax.broadcasted_iota(jnp.int32, st.shape, 0) + diag * tk
            c = lax.broadcasted_iota(jnp.int32, st.shape, 1) + lo_q
            st = jnp.where(r <= c, st, -jnp.inf)
        m_old = m_ref[:, lo_q:]
        m_new = jnp.maximum(m_old, jnp.max(st, axis=0, keepdims=True))
        alpha = jnp.exp2(m_old - m_new)
        p = jnp.exp2(st - m_new)
        m_ref[:, lo_q:] = m_new
        acc_ref[:, lo_q:] = alpha * acc_ref[:, lo_q:] + _dot(vt_ref[j], p.astype(BF16))

    n_tiles = S // tq
    for t in range(n_tiles):
        p0_ref[t] = dead_pairs(t)

    def q_tile(i, qa_ref, qa_next_ref):
        q0 = pl.multiple_of(i * tq, tq)
        p0 = p0_ref[i]

        def pairs(n_pairs, first_pair):
            def body(it, _):
                for u in range(n_pairs):
                    j = 2 * (first_pair + n_pairs * it + u)
                    logits(j + 1, s1_ref, 0, qa_ref)
                    consume(j, s0_ref, None, 0)
                    logits(j + 2, s0_ref, 0, qa_ref)
                    consume(j + 1, s1_ref, None, 0)
                return 0
            return body

        n_pairs = i - p0
        lax.fori_loop(0, n_pairs // 2, pairs(2, p0), 0)
        lax.fori_loop(0, n_pairs % 2, pairs(1, p0 + 2 * (n_pairs // 2)), 0)
        logits(2 * i + 1, s1_ref, tk, qa_ref)
        i_next = jnp.minimum(i + 1, n_tiles - 1)
        stage(i_next, qa_next_ref)
        consume(2 * i, s0_ref, 0, 0)
        logits(2 * p0_ref[i_next], s0_ref, 0, qa_next_ref)
        consume(2 * i + 1, s1_ref, 1, tk)
        o = (acc_ref[0:dh, :] / acc_ref[dh:dh + 1, :]).T
        o_ref[pl.ds(q0, tq), :] = (o * _silu(gate_ref[pl.ds(q0, tq), :].astype(F32))).astype(o_ref.dtype)
        reset_stats()

    def tile_pair(it, _):
        q_tile(2 * it, qa0_ref, qa1_ref)
        q_tile(2 * it + 1, qa1_ref, qa0_ref)
        return 0

    assert n_tiles % 2 == 0, "query tiles alternate between two staging buffers"
    stage(0, qa0_ref)
    reset_stats()
    logits(2 * p0_ref[0], s0_ref, 0, qa0_ref)
    lax.fori_loop(0, n_tiles // 2, tile_pair, 0)


def _fox_attention(u, cum, *, B, S, H, E, tq=1024, tk=512, ch=512):
    T = u.shape[0]
    dh = FOX_HEAD_DIM
    cum4 = cum.reshape(B, H, S // ch, ch)
    body = functools.partial(_fox_attn_body, S=S, tq=tq, tk=tk, ch=ch, scale=dh ** -0.5)
    return pl.pallas_call(
        body,
        grid=(B, H),
        in_specs=[
            pl.BlockSpec((S, dh), lambda b, h: (b, h)),
            pl.BlockSpec((S, dh), lambda b, h: (b, H + h)),
            pl.BlockSpec((S, dh), lambda b, h: (b, 2 * H + h)),
            pl.BlockSpec((S, dh), lambda b, h: (b, 3 * H + h)),
            pl.BlockSpec((1, 1, S // ch, ch), lambda b, h: (b, h, 0, 0)),
        ],
        out_specs=pl.BlockSpec((S, dh), lambda b, h: (b, h)),
        out_shape=jax.ShapeDtypeStruct((T, E), BF16),
        scratch_shapes=[pltpu.VMEM((S // tk, dh + ONES_ROWS, tk), BF16), pltpu.VMEM((S, 2 * dh), BF16),
                        pltpu.VMEM((S, LANES), F32), pltpu.VMEM((tk, tq), F32), pltpu.VMEM((tk, tq), F32),
                        pltpu.VMEM((tq, 2 * dh), BF16), pltpu.VMEM((tq, 2 * dh), BF16), pltpu.VMEM((1, tq), F32),
                        pltpu.VMEM((dh + ONES_ROWS, tq), F32), pltpu.SMEM((S // tq,), jnp.int32)],
        compiler_params=_params("parallel", "parallel"),
        name="fox_attention",
    )(u, u, u, u, cum4)


def _fox_layer(x, g, w_in, b_f, w_out, *, B, S):
    E = w_out.shape[0]
    H = E // FOX_HEAD_DIM
    w_main = w_in[:, :3 * E]
    w_f = w_in[:, 3 * E:3 * E + H]
    w_gate = w_in[:, 3 * E + H:]
    w_cat = jnp.concatenate([w_main, w_gate], axis=1).astype(BF16)
    wf_pad = jnp.pad(w_f, ((0, 0), (0, LANES - H))).astype(BF16)
    bf_pad = jnp.pad(b_f, (0, LANES - H)).reshape(1, LANES)
    u = _rms_matmul(x, g, w_cat)
    cum = _fox_cum(x, g, wf_pad, bf_pad, B=B, S=S, H=H)
    o = _fox_attention(u, cum, B=B, S=S, H=H, E=E)
    return _out_proj(o, w_out.astype(BF16), x)


def _gla_in_proj_body(x_ref, g_ref, w_ref, w1_ref, w2_ref, b_ref, o_ref, d_ref, hn_ref, low_ref):
    @pl.when(pl.program_id(1) == 0)
    def _():
        hn = _rms(x_ref[...], g_ref[...]).astype(BF16)
        hn_ref[...] = hn
        low_ref[...] = _dot(hn, w1_ref[...]).astype(BF16)

    o_ref[...] = _dot(hn_ref[...], w_ref[...]).astype(o_ref.dtype)
    d_ref[...] = _log_sigmoid(_dot(low_ref[...], w2_ref[...]) + b_ref[...]) * (1.0 / GLA_GATE_NORM)


def _gla_in_proj(x, g, w, w1p, w2p, b_g, *, tm=1024, nj=2):
    T, D = x.shape
    N, KD = w.shape[1], w2p.shape[1]
    tn, td = N // nj, KD // nj
    assert T % tm == 0 and tn % LANES == 0 and td % LANES == 0
    return pl.pallas_call(
        _gla_in_proj_body,
        grid=(T // tm, nj),
        in_specs=[
            pl.BlockSpec((tm, D), lambda i, j: (i, 0)),
            pl.BlockSpec((1, D), lambda i, j: (0, 0)),
            pl.BlockSpec((D, tn), lambda i, j: (0, j)),
            pl.BlockSpec((D, LANES), lambda i, j: (0, 0)),
            pl.BlockSpec((LANES, td), lambda i, j: (0, j)),
            pl.BlockSpec((1, td), lambda i, j: (0, j)),
        ],
        out_specs=[pl.BlockSpec((tm, tn), lambda i, j: (i, j)), pl.BlockSpec((tm, td), lambda i, j: (i, j))],
        out_shape=[jax.ShapeDtypeStruct((T, N), BF16), jax.ShapeDtypeStruct((T, KD), F32)],
        scratch_shapes=[pltpu.VMEM((tm, D), BF16), pltpu.VMEM((tm, LANES), BF16)],
        compiler_params=_params("parallel", "arbitrary"),
        name="gla_in_proj",
    )(x, g, w, w1p, w2p, b_g)


def _gla_chunk_body(q_ref, k_ref, v_ref, gate_ref, g_ref, gn_ref, o_ref, st_ref, *, tb, dk, dv, scale):
    C = GLA_CHUNK

    @pl.when(pl.program_id(1) == 0)
    def _():
        st_ref[...] = jnp.zeros_like(st_ref)

    r = lax.broadcasted_iota(jnp.int32, (tb, tb), 0)
    c = lax.broadcasted_iota(jnp.int32, (tb, tb), 1)
    shift = C.bit_length() - 1
    same = jnp.right_shift(r, shift) == jnp.right_shift(c, shift)
    in_chunk_causal = jnp.logical_and(same, c <= r)
    tril_blk = jnp.where(in_chunk_causal, 1.0, 0.0).astype(BF16)
    b_all = _dot_01(tril_blk, g_ref[...])
    n_chunks = tb // C

    o_intra = []
    for h in range(GLA_HEADS):
        ks_, vs_ = slice(h * dk, (h + 1) * dk), slice(h * dv, (h + 1) * dv)
        b = b_all[:, ks_]
        b_mid = jnp.concatenate([jnp.broadcast_to(b[n * C + C // 2:n * C + C // 2 + 1, :], (C, dk))
                                 for n in range(n_chunks)], axis=0)
        qs = (q_ref[:, ks_].astype(F32) * scale * jnp.exp(b - b_mid)).astype(BF16)
        ks = (k_ref[:, ks_].astype(F32) * jnp.exp(b_mid - b)).astype(BF16)
        scores = jnp.where(in_chunk_causal, _dot_nt(qs, ks), 0.0)
        o_intra.append(_dot(scores.astype(BF16), v_ref[:, vs_]))

    for n in range(n_chunks):
        rows = slice(n * C, (n + 1) * C)
        for h in range(GLA_HEADS):
            ks_, vs_ = slice(h * dk, (h + 1) * dk), slice(h * dv, (h + 1) * dv)
            qc = q_ref[rows, ks_].astype(F32) * scale
            kc = k_ref[rows, ks_].astype(F32)
            vc = v_ref[rows, vs_]
            b = b_all[rows, ks_]
            b_last = b[C - 1:C, :]
            st = st_ref[h]
            o = _dot_nt((qc * jnp.exp(b)).astype(BF16), st.astype(BF16)) + o_intra[h][rows, :]
            k_dec = (kc * jnp.exp(b_last - b)).astype(BF16)
            st_ref[h] = st * jnp.exp(b_last) + _dot_tn(vc, k_dec)
            gated = _rms(o, gn_ref[...]) * _silu(gate_ref[rows, vs_].astype(F32))
            o_ref[rows, vs_] = gated.astype(o_ref.dtype)


def _gla_chunks(u, gdec, gn_g, *, B, S, E, tb=256):
    T = u.shape[0]
    Hh = GLA_HEADS
    KD = gdec.shape[1]
    dk, dv = KD // Hh, E // Hh
    nS = S // tb
    body = functools.partial(_gla_chunk_body, tb=tb, dk=dk, dv=dv, scale=dk ** -0.5)
    return pl.pallas_call(
        body,
        grid=(B, nS),
        in_specs=[
            pl.BlockSpec((tb, KD), lambda b, i: (b * nS + i, 0)),
            pl.BlockSpec((tb, KD), lambda b, i: (b * nS + i, 1)),
            pl.BlockSpec((tb, E), lambda b, i: (b * nS + i, (2 * KD) // E)),
            pl.BlockSpec((tb, E), lambda b, i: (b * nS + i, (2 * KD + E) // E)),
            pl.BlockSpec((tb, KD), lambda b, i: (b * nS + i, 0)),
            pl.BlockSpec((1, dv), lambda b, i: (0, 0)),
        ],
        out_specs=pl.BlockSpec((tb, E), lambda b, i: (b * nS + i, 0)),
        out_shape=jax.ShapeDtypeStruct((T, E), BF16),
        scratch_shapes=[pltpu.VMEM((Hh, dv, dk), F32)],
        compiler_params=_params("parallel", "arbitrary"),
        name="gla_chunks",
    )(u, u, u, u, gdec, gn_g)


def _gla_layer(x, g, w_in, w_g1, w_g2, b_g, gn_g, w_out, *, B, S):
    E = w_out.shape[0]
    KD = w_g2.shape[1]
    rank = w_g1.shape[1]
    w1p = jnp.pad(w_g1, ((0, 0), (0, LANES - rank))).astype(BF16)
    w2p = jnp.pad(w_g2, ((0, LANES - rank), (0, 0))).astype(BF16)
    u, gdec = _gla_in_proj(x, g, w_in.astype(BF16), w1p, w2p, b_g.reshape(1, KD))
    o = _gla_chunks(u, gdec, gn_g.reshape(1, -1), B=B, S=S, E=E)
    return _out_proj(o, w_out.astype(BF16), x)


def _lru_body(xb_ref, xh_ref, gate_ref, x_ref, wc_ref, bc_ref, wax_ref, ba_ref, bx_ref, lam_ref,
              wout_ref, fg_ref, y_ref, buf_ref, a_ref, b_ref, h_ref, carry_ref, *, tm, E):
    i = pl.program_id(1)

    @pl.when(i == 0)
    def _():
        carry_ref[...] = jnp.zeros_like(carry_ref)

    buf_ref[0:LRU_HALO, :] = jnp.where(i > 0, xh_ref[...].astype(F32), 0.0)
    buf_ref[LRU_HALO:, :] = xb_ref[...].astype(F32)
    off = LRU_HALO - (LRU_CONV_K - 1)
    xc = jnp.broadcast_to(bc_ref[...], (tm, E))
    for k in range(LRU_CONV_K):
        xc = xc + wc_ref[k:k + 1, :] * buf_ref[off + k:off + k + tm, :]

    lam = lam_ref[...]
    sp = jnp.maximum(-lam, 0.0) + jnp.log(1.0 + jnp.exp(-jnp.abs(lam)))
    rate = sp * (-LRU_C * 1.4426950408889634)
    W = LRU_BLOCK_W
    for n in range(E // W):
        sl = slice(n * W, (n + 1) * W)
        xn = xc[:, sl]
        pre = _dot(xn.astype(BF16), wax_ref[n])
        r = _sigmoid(pre[:, :W] + ba_ref[:, sl])
        ig = _sigmoid(pre[:, W:] + bx_ref[:, sl])
        a = jnp.exp2(r * rate[:, sl])
        beta = jnp.sqrt(1.0 - a * a)
        a_ref[:, sl] = a
        b_ref[:, sl] = beta * (ig * xn)

    def step(t, h):
        h = a_ref[pl.ds(t, 1), :] * h + b_ref[pl.ds(t, 1), :]
        h_ref[pl.ds(t, 1), :] = h
        return h

    carry_ref[...] = lax.fori_loop(0, tm, step, carry_ref[...], unroll=8)

    og = (h_ref[...] * _silu(gate_ref[...].astype(F32))).astype(BF16)
    y = x_ref[...] + _dot(og, wout_ref[...])
    y_ref[...] = _rms(y, fg_ref[...])


def _lru_layer(x, g, w_in, w_conv, b_conv, w_a, b_a, w_x, b_x, lam, w_out, final_g, *, B, S, tm=256):
    T, D = x.shape
    E = w_out.shape[0]
    u = _rms_matmul(x, g, w_in.astype(BF16))
    wax = jnp.concatenate([w_a, w_x], axis=-1).astype(BF16)
    nS = S // tm
    hpb = tm // LRU_HALO

    def row(n):
        return pl.BlockSpec((1, n), lambda b, i: (0, 0))

    body = functools.partial(_lru_body, tm=tm, E=E)
    return pl.pallas_call(
        body,
        grid=(B, nS),
        in_specs=[
            pl.BlockSpec((tm, E), lambda b, i: (b * nS + i, 0)),
            pl.BlockSpec((LRU_HALO, E), lambda b, i: (jnp.maximum((b * nS + i) * hpb - 1, 0), 0)),
            pl.BlockSpec((tm, E), lambda b, i: (b * nS + i, 1)),
            pl.BlockSpec((tm, D), lambda b, i: (b * nS + i, 0)),
            pl.BlockSpec((LRU_CONV_K, E), lambda b, i: (0, 0)),
            row(E),
            pl.BlockSpec(wax.shape, lambda b, i: (0, 0, 0)),
            row(E), row(E), row(E),
            pl.BlockSpec((E, D), lambda b, i: (0, 0)),
            row(D),
        ],
        out_specs=pl.BlockSpec((tm, D), lambda b, i: (b * nS + i, 0)),
        out_shape=jax.ShapeDtypeStruct((T, D), F32),
        scratch_shapes=[pltpu.VMEM((tm + LRU_HALO, E), F32), pltpu.VMEM((tm, E), F32),
                        pltpu.VMEM((tm, E), F32), pltpu.VMEM((tm, E), F32), pltpu.VMEM((1, E), F32)],
        compiler_params=_params("parallel", "arbitrary"),
        name="rglru_mixer",
    )(u, u, u, x, w_conv, b_conv.reshape(1, E), wax, b_a.reshape(1, E), b_x.reshape(1, E),
      lam.reshape(1, E), w_out.astype(BF16), final_g.reshape(1, D))


def kernel(x, norm_g, final_g, conv_w_in, conv_w_dw, conv_b_dw, conv_ln_g, conv_ln_b, conv_w_out, fox_w_in, fox_b_f, fox_w_out, gla_w_in, gla_w_g1, gla_w_g2, gla_b_g, gla_gn_g, gla_w_out, lru_w_in, lru_w_conv, lru_b_conv, lru_w_a, lru_b_a, lru_w_x, lru_b_x, lru_lam, lru_w_out):
    B, S, D = x.shape
    assert norm_g.shape[0] == 4 and conv_w_in.shape[0] == 1, "one layer per mixer"
    E = conv_w_out.shape[1]
    h = x.reshape(B * S, D)
    ng = norm_g.reshape(4, 1, D)
    h = _conv_layer(h, ng[0], conv_w_in[0].astype(BF16), conv_w_dw[0], conv_b_dw[0].reshape(1, E),
                    conv_ln_g[0].reshape(1, E), conv_ln_b[0].reshape(1, E), conv_w_out[0].astype(BF16), B=B, S=S)
    h = _fox_layer(h, ng[1], fox_w_in[0], fox_b_f[0], fox_w_out[0], B=B, S=S)
    h = _gla_layer(h, ng[2], gla_w_in[0], gla_w_g1[0], gla_w_g2[0], gla_b_g[0], gla_gn_g[0], gla_w_out[0], B=B, S=S)
    h = _lru_layer(h, ng[3], lru_w_in[0], lru_w_conv[0], lru_b_conv[0], lru_w_a[0], lru_b_a[0], lru_w_x[0],
                   lru_b_x[0], lru_lam[0], lru_w_out[0], final_g, B=B, S=S)
    return h.reshape(B, S, D)
```

```python
import functools

import jax
import jax.numpy as jnp
from jax import lax
from jax.experimental import pallas as pl
from jax.experimental.pallas import tpu as pltpu

F32 = jnp.float32
BF16 = jnp.bfloat16
EPS = 1e-6

V7X_VMEM_BYTES = 64 * 1024 * 1024
VMEM_LIMIT_BYTES = V7X_VMEM_BYTES - 16 * 1024 * 1024
LANES = 128
SUBLANES = 8

CONV_K = 31
CONV_HALO = 32
FOX_HEAD_DIM = 128
ONES_ROWS = 16
UNDERFLOW_BITS = 176.0
NORM_SLACK = 1.01
GLA_HEADS = 4
GLA_CHUNK = 64
GLA_GATE_NORM = 16.0
LRU_CONV_K = 4
LRU_HALO = 16
LRU_BLOCK_W = 128
LRU_C = 8.0


def _params(*sem):
    return pltpu.CompilerParams(dimension_semantics=sem, vmem_limit_bytes=VMEM_LIMIT_BYTES)


NEG_LOG2E = -1.4426950408889634


def _exp_neg(x):
    return jnp.exp2(x * NEG_LOG2E)


def _sigmoid(x):
    return 1.0 / (1.0 + _exp_neg(x))


def _silu(x):
    return x * _sigmoid(x)


def _log_sigmoid(x):
    return jnp.minimum(x, 0.0) - jnp.log(1.0 + _exp_neg(jnp.abs(x)))


def _rms(x, g):
    return x * lax.rsqrt(jnp.mean(x * x, axis=-1, keepdims=True) + EPS) * g


def _dot(a, b):
    return jnp.dot(a, b, preferred_element_type=F32)


def _dot_nt(a, b):
    return lax.dot_general(a, b, (((1,), (1,)), ((), ())), preferred_element_type=F32)


def _dot_tn(a, b):
    return lax.dot_general(a, b, (((0,), (0,)), ((), ())), preferred_element_type=F32)


def _dot_01(m01, x):
    x1 = x.astype(BF16)
    r1 = x - x1.astype(F32)
    x2 = r1.astype(BF16)
    x3 = (r1 - x2.astype(F32)).astype(BF16)
    return _dot(m01, x1) + _dot(m01, x2) + _dot(m01, x3)


def _rms_matmul_body(x_ref, g_ref, w_ref, o_ref, hn_ref):
    @pl.when(pl.program_id(1) == 0)
    def _():
        hn_ref[...] = _rms(x_ref[...], g_ref[...]).astype(BF16)

    o_ref[...] = _dot(hn_ref[...], w_ref[...]).astype(o_ref.dtype)


def _rms_matmul(x, g, w, *, tm=1024, tn=2048):
    T, D = x.shape
    N = w.shape[1]
    assert T % tm == 0 and N % tn == 0
    return pl.pallas_call(
        _rms_matmul_body,
        grid=(T // tm, N // tn),
        in_specs=[
            pl.BlockSpec((tm, D), lambda i, j: (i, 0)),
            pl.BlockSpec((1, D), lambda i, j: (0, 0)),
            pl.BlockSpec((D, tn), lambda i, j: (0, j)),
        ],
        out_specs=pl.BlockSpec((tm, tn), lambda i, j: (i, j)),
        out_shape=jax.ShapeDtypeStruct((T, N), BF16),
        scratch_shapes=[pltpu.VMEM((tm, D), BF16)],
        compiler_params=_params("parallel", "arbitrary"),
        name="rms_in_proj",
    )(x, g, w)


def _out_proj_body(og_ref, w_ref, x_ref, y_ref):
    y_ref[...] = x_ref[...] + _dot(og_ref[...], w_ref[...])


def _out_proj(og, w_out, x, *, tm=1024):
    T, E = og.shape
    D = x.shape[1]
    assert T % tm == 0
    return pl.pallas_call(
        _out_proj_body,
        grid=(T // tm,),
        in_specs=[
            pl.BlockSpec((tm, E), lambda i: (i, 0)),
            pl.BlockSpec((E, D), lambda i: (0, 0)),
            pl.BlockSpec((tm, D), lambda i: (i, 0)),
        ],
        out_specs=pl.BlockSpec((tm, D), lambda i: (i, 0)),
        out_shape=jax.ShapeDtypeStruct((T, D), F32),
        compiler_params=_params("parallel"),
        name="out_proj",
    )(og, w_out, x)


def _conv_body(a_ref, b_ref, gate_ref, ah_ref, bh_ref, x_ref, wdw_ref, bdw_ref, lng_ref, lnb_ref,
               wout_ref, y_ref, sh_ref, acc_ref, *, tm, E, rows, cols):
    i = pl.program_id(1)
    off = CONV_HALO - (CONV_K - 1)
    n_sh = tm + CONV_HALO - SUBLANES
    for c0 in range(0, E, cols):
        cs = slice(c0, c0 + cols)
        vh = ah_ref[:, cs].astype(F32) * _sigmoid(bh_ref[:, cs].astype(F32))
        sh_ref[0, 0:CONV_HALO, :] = jnp.where(i > 0, vh, 0.0)
        sh_ref[0, CONV_HALO:, :] = a_ref[:, cs].astype(F32) * _sigmoid(b_ref[:, cs].astype(F32))
        for s in range(1, SUBLANES):
            sh_ref[s, 0:n_sh, :] = sh_ref[0, s:s + n_sh, :]
        for r0 in range(0, tm, rows):
            acc = jnp.broadcast_to(bdw_ref[:, cs], (rows, cols))
            for k in range(CONV_K):
                s, a8 = (off + k) % SUBLANES, (off + k) // SUBLANES * SUBLANES
                wk = jnp.concatenate([wdw_ref[k, :, cs]] * (rows // SUBLANES), axis=0)
                acc = acc + wk * sh_ref[s, r0 + a8:r0 + a8 + rows, :]
            acc_ref[r0:r0 + rows, cs] = acc

    c = acc_ref[...]
    mu = jnp.mean(c, axis=-1, keepdims=True)
    d = c - mu
    var = jnp.mean(d * d, axis=-1, keepdims=True)
    ln = d * lax.rsqrt(var + EPS) * lng_ref[...] + lnb_ref[...]
    og = (_silu(ln) * _silu(gate_ref[...].astype(F32))).astype(BF16)
    y_ref[...] = x_ref[...] + _dot(og, wout_ref[...])


def _conv_layer(x, g, w_in, w_dw, b_dw, ln_g, ln_b, w_out, *, B, S, tm=512):
    T, D = x.shape
    E = w_out.shape[0]
    u = _rms_matmul(x, g, w_in)
    nS = S // tm
    hpb = tm // CONV_HALO

    def tile(col):
        return pl.BlockSpec((tm, E), lambda b, i: (b * nS + i, col))

    def halo(col):
        return pl.BlockSpec((CONV_HALO, E), lambda b, i: (jnp.maximum((b * nS + i) * hpb - 1, 0), col))

    def row(n):
        return pl.BlockSpec((1, n), lambda b, i: (0, 0))

    cols = 512
    body = functools.partial(_conv_body, tm=tm, E=E, rows=32, cols=cols)
    return pl.pallas_call(
        body,
        grid=(B, nS),
        in_specs=[tile(0), tile(1), tile(2), halo(0), halo(1),
                  pl.BlockSpec((tm, D), lambda b, i: (b * nS + i, 0)),
                  pl.BlockSpec((CONV_K, SUBLANES, E), lambda b, i: (0, 0, 0)),
                  row(E), row(E), row(E),
                  pl.BlockSpec((E, D), lambda b, i: (0, 0))],
        out_specs=pl.BlockSpec((tm, D), lambda b, i: (b * nS + i, 0)),
        out_shape=jax.ShapeDtypeStruct((T, D), F32),
        scratch_shapes=[pltpu.VMEM((SUBLANES, tm + CONV_HALO, cols), F32), pltpu.VMEM((tm, E), F32)],
        compiler_params=_params("parallel", "parallel"),
        name="conv_mixer",
    )(u, u, u, u, u, x, jnp.broadcast_to(w_dw[:, None, :], (CONV_K, SUBLANES, E)), b_dw, ln_g, ln_b, w_out)


def _fox_cum_body(x_ref, g_ref, wf_ref, bf_ref, o_ref, carry_ref, *, tc, H):
    @pl.when(pl.program_id(1) == 0)
    def _():
        carry_ref[...] = jnp.zeros_like(carry_ref)

    hn = _rms(x_ref[...], g_ref[...]).astype(BF16)
    log_f = _log_sigmoid(_dot(hn, wf_ref[...]) + bf_ref[...])
    r = lax.broadcasted_iota(jnp.int32, (tc, tc), 0)
    c = lax.broadcasted_iota(jnp.int32, (tc, tc), 1)
    tril = jnp.where(c <= r, 1.0, 0.0).astype(BF16)
    cum = _dot_01(tril, log_f) + carry_ref[...]
    carry_ref[...] = cum[tc - 1:tc, :]
    o_ref[0] = cum.T[0:H, :]


def _fox_cum(x, g, wf, bfp, *, B, S, H, tc=512):
    T, D = x.shape
    nS = S // tc
    body = functools.partial(_fox_cum_body, tc=tc, H=H)
    return pl.pallas_call(
        body,
        grid=(B, nS),
        in_specs=[
            pl.BlockSpec((tc, D), lambda b, i: (b * nS + i, 0)),
            pl.BlockSpec((1, D), lambda b, i: (0, 0)),
            pl.BlockSpec((D, LANES), lambda b, i: (0, 0)),
            pl.BlockSpec((1, LANES), lambda b, i: (0, 0)),
        ],
        out_specs=pl.BlockSpec((1, H, tc), lambda b, i: (b, 0, i)),
        out_shape=jax.ShapeDtypeStruct((B, H, S), F32),
        scratch_shapes=[pltpu.VMEM((1, LANES), F32)],
        compiler_params=_params("parallel", "arbitrary"),
        name="fox_forget_cumsum",
    )(x, g, wf, bfp)


def _fox_attn_body(q_ref, k_ref, v_ref, gate_ref, c_ref, o_ref, vt_ref, ka_ref, cc_ref, s0_ref, s1_ref, qa0_ref, qa1_ref,
                   m_ref, acc_ref, p0_ref, *, S, tq, tk, ch, scale):
    dh = q_ref.shape[1]
    log2e = 1.4426950408889634
    assert tq == 2 * tk, "a query tile spans exactly two kv blocks (pairwise pipelined loop)"
    lane = lax.broadcasted_iota(jnp.int32, (1, LANES), 1)

    def split3(x):
        hi = x.astype(BF16).astype(F32)
        mid = (x - hi).astype(BF16).astype(F32)
        lo = (x - hi - mid).astype(BF16).astype(F32)
        return hi, mid, lo

    ones_sq = jnp.ones((dh, LANES), BF16)

    def max_row_norm2(x):
        rs = _dot((x * x).astype(BF16), ones_sq)
        return jnp.max(rs, axis=0, keepdims=True)

    sel_r = lax.broadcasted_iota(jnp.int32, (SUBLANES, LANES), 0)
    sel_l = lax.broadcasted_iota(jnp.int32, (SUBLANES, LANES), 1)
    sel_sum = jnp.where(sel_r < 3, 1.0, 0.0)
    sel_aug = jnp.where(jnp.logical_and(sel_r < 3, sel_l == sel_r), -1.0,
                        jnp.where(jnp.logical_and(sel_r == 3, jnp.logical_and(sel_l >= 3, sel_l < 6)), 1.0, 0.0))

    kn2 = jnp.zeros((1, LANES), F32)
    for n0 in range(S // ch):
        rows = slice(n0 * ch, (n0 + 1) * ch)
        kn2 = jnp.maximum(kn2, max_row_norm2(k_ref[rows, :].astype(F32)))
        vt = v_ref[rows, :].astype(F32).T.astype(BF16)
        for s0 in range(ch // tk):
            vt_ref[n0 * (ch // tk) + s0, 0:dh, :] = vt[:, s0 * tk:(s0 + 1) * tk]
            vt_ref[n0 * (ch // tk) + s0, dh:, :] = jnp.ones((ONES_ROWS, tk), BF16)
        hi, mid, lo = split3(c_ref[0, 0, n0:n0 + 1, :] * log2e)
        parts = jnp.concatenate([hi, mid, lo, jnp.ones((1, ch), F32), jnp.zeros((SUBLANES - 4, ch), F32)], axis=0)
        cc_ref[rows, :] = _dot_tn(parts, sel_sum)
        ka_ref[rows, 0:dh] = k_ref[rows, :]
        ka_ref[rows, dh:] = _dot_tn(parts, sel_aug).astype(BF16)
    c_end = cc_ref[pl.ds(tk - 1, S // tk, stride=tk), :]

    def scaled_q(i):
        q0 = pl.multiple_of(i * tq, tq)
        return (q_ref[pl.ds(q0, tq), :].astype(F32) * (scale * log2e)).astype(BF16)

    def dead_pairs(i):
        q0 = pl.multiple_of(i * tq, tq)
        qw = scaled_q(i)
        c0 = cc_ref[pl.ds(q0, 1), :]
        qk = jnp.sqrt(max_row_norm2(qw.astype(F32)) * kn2) * NORM_SLACK
        n_dead = jnp.sum(jnp.where(c_end > c0 + (UNDERFLOW_BITS + 2.0 * qk), 1, 0)[:, 0:1])
        return jnp.minimum(n_dead // 2, i)

    def stage(i, qa_ref):
        hi, mid, lo = split3(cc_ref[pl.ds(pl.multiple_of(i * tq, tq), 1), :])
        aug = jnp.where(lane < 3, 1.0, jnp.where(lane == 3, hi, jnp.where(lane == 4, mid,
                        jnp.where(lane == 5, lo, 0.0))))
        qa_ref[:, 0:dh] = scaled_q(i)
        qa_ref[:, dh:] = jnp.broadcast_to(aug, (tq, LANES)).astype(BF16)

    def reset_stats():
        m_ref[...] = jnp.full((1, tq), -jnp.inf, F32)
        acc_ref[...] = jnp.zeros_like(acc_ref)

    def logits(j, s_ref, lo_q, qa_ref):
        kv0 = pl.multiple_of(j * tk, tk)
        s_ref[:, lo_q:] = _dot_nt(ka_ref[pl.ds(kv0, tk), :], qa_ref[lo_q:, :])

    def consume(j, s_ref, diag, lo_q):
        st = s_ref[:, lo_q:]
        if diag is not None:
            r = lax.broadcasted_iota(jnp.int32, st.shape, 0) + diag * tk
            c = lax.broadcasted_iota(jnp.int32, st.shape, 1) + lo_q
            st = jnp.where(r <= c, st, -jnp.inf)
        m_old = m_ref[:, lo_q:]
        m_new = jnp.maximum(m_old, jnp.max(st, axis=0, keepdims=True))
        alpha = jnp.exp2(m_old - m_new)
        p = jnp.exp2(st - m_new)
        m_ref[:, lo_q:] = m_new
        acc_ref[:, lo_q:] = alpha * acc_ref[:, lo_q:] + _dot(vt_ref[j], p.astype(BF16))

    n_tiles = S // tq
    for t in range(n_tiles):
        p0_ref[t] = dead_pairs(t)

    def q_tile(i, qa_ref, qa_next_ref):
        q0 = pl.multiple_of(i * tq, tq)
        p0 = p0_ref[i]

        def pairs(n_pairs, first_pair):
            def body(it, _):
                for u in range(n_pairs):
                    j = 2 * (first_pair + n_pairs * it + u)
                    logits(j + 1, s1_ref, 0, qa_ref)
                    consume(j, s0_ref, None, 0)
                    logits(j + 2, s0_ref, 0, qa_ref)
                    consume(j + 1, s1_ref, None, 0)
                return 0
            return body

        n_pairs = i - p0
        lax.fori_loop(0, n_pairs // 2, pairs(2, p0), 0)
        lax.fori_loop(0, n_pairs % 2, pairs(1, p0 + 2 * (n_pairs // 2)), 0)
        logits(2 * i + 1, s1_ref, tk, qa_ref)
        i_next = jnp.minimum(i + 1, n_tiles - 1)
        stage(i_next, qa_next_ref)
        consume(2 * i, s0_ref, 0, 0)
        logits(2 * p0_ref[i_next], s0_ref, 0, qa_next_ref)
        consume(2 * i + 1, s1_ref, 1, tk)
        o = (acc_ref[0:dh, :] / acc_ref[dh:dh + 1, :]).T
        o_ref[pl.ds(q0, tq), :] = (o * _silu(gate_ref[pl.ds(q0, tq), :].astype(F32))).astype(o_ref.dtype)
        reset_stats()

    def tile_pair(it, _):
        q_tile(2 * it, qa0_ref, qa1_ref)
        q_tile(2 * it + 1, qa1_ref, qa0_ref)
        return 0

    assert n_tiles % 2 == 0, "query tiles alternate between two staging buffers"
    stage(0, qa0_ref)
    reset_stats()
    logits(2 * p0_ref[0], s0_ref, 0, qa0_ref)
    lax.fori_loop(0, n_tiles // 2, tile_pair, 0)


def _fox_attention(u, cum, *, B, S, H, E, tq=1024, tk=512, ch=512):
    T = u.shape[0]
    dh = FOX_HEAD_DIM
    cum4 = cum.reshape(B, H, S // ch, ch)
    body = functools.partial(_fox_attn_body, S=S, tq=tq, tk=tk, ch=ch, scale=dh ** -0.5)
    return pl.pallas_call(
        body,
        grid=(B, H),
        in_specs=[
            pl.BlockSpec((S, dh), lambda b, h: (b, h)),
            pl.BlockSpec((S, dh), lambda b, h: (b, H + h)),
            pl.BlockSpec((S, dh), lambda b, h: (b, 2 * H + h)),
            pl.BlockSpec((S, dh), lambda b, h: (b, 3 * H + h)),
            pl.BlockSpec((1, 1, S // ch, ch), lambda b, h: (b, h, 0, 0)),
        ],
        out_specs=pl.BlockSpec((S, dh), lambda b, h: (b, h)),
        out_shape=jax.ShapeDtypeStruct((T, E), BF16),
        scratch_shapes=[pltpu.VMEM((S // tk, dh + ONES_ROWS, tk), BF16), pltpu.VMEM((S, 2 * dh), BF16),
                        pltpu.VMEM((S, LANES), F32), pltpu.VMEM((tk, tq), F32), pltpu.VMEM((tk, tq), F32),
                        pltpu.VMEM((tq, 2 * dh), BF16), pltpu.VMEM((tq, 2 * dh), BF16), pltpu.VMEM((1, tq), F32),
                        pltpu.VMEM((dh + ONES_ROWS, tq), F32), pltpu.SMEM((S // tq,), jnp.int32)],
        compiler_params=_params("parallel", "parallel"),
        name="fox_attention",
    )(u, u, u, u, cum4)


def _fox_layer(x, g, w_in, b_f, w_out, *, B, S):
    E = w_out.shape[0]
    H = E // FOX_HEAD_DIM
    w_main = w_in[:, :3 * E]
    w_f = w_in[:, 3 * E:3 * E + H]
    w_gate = w_in[:, 3 * E + H:]
    w_cat = jnp.concatenate([w_main, w_gate], axis=1).astype(BF16)
    wf_pad = jnp.pad(w_f, ((0, 0), (0, LANES - H))).astype(BF16)
    bf_pad = jnp.pad(b_f, (0, LANES - H)).reshape(1, LANES)
    u = _rms_matmul(x, g, w_cat)
    cum = _fox_cum(x, g, wf_pad, bf_pad, B=B, S=S, H=H)
    o = _fox_attention(u, cum, B=B, S=S, H=H, E=E)
    return _out_proj(o, w_out.astype(BF16), x)


def _gla_in_proj_body(x_ref, g_ref, w_ref, w1_ref, w2_ref, b_ref, o_ref, d_ref, hn_ref, low_ref):
    @pl.when(pl.program_id(1) == 0)
    def _():
        hn = _rms(x_ref[...], g_ref[...]).astype(BF16)
        hn_ref[...] = hn
        low_ref[...] = _dot(hn, w1_ref[...]).astype(BF16)

    o_ref[...] = _dot(hn_ref[...], w_ref[...]).astype(o_ref.dtype)
    d_ref[...] = _log_sigmoid(_dot(low_ref[...], w2_ref[...]) + b_ref[...]) * (1.0 / GLA_GATE_NORM)


def _gla_in_proj(x, g, w, w1p, w2p, b_g, *, tm=1024, nj=2):
    T, D = x.shape
    N, KD = w.shape[1], w2p.shape[1]
    tn, td = N // nj, KD // nj
    assert T % tm == 0 and tn % LANES == 0 and td % LANES == 0
    return pl.pallas_call(
        _gla_in_proj_body,
        grid=(T // tm, nj),
        in_specs=[
            pl.BlockSpec((tm, D), lambda i, j: (i, 0)),
            pl.BlockSpec((1, D), lambda i, j: (0, 0)),
            pl.BlockSpec((D, tn), lambda i, j: (0, j)),
            pl.BlockSpec((D, LANES), lambda i, j: (0, 0)),
            pl.BlockSpec((LANES, td), lambda i, j: (0, j)),
            pl.BlockSpec((1, td), lambda i, j: (0, j)),
        ],
        out_specs=[pl.BlockSpec((tm, tn), lambda i, j: (i, j)), pl.BlockSpec((tm, td), lambda i, j: (i, j))],
        out_shape=[jax.ShapeDtypeStruct((T, N), BF16), jax.ShapeDtypeStruct((T, KD), F32)],
        scratch_shapes=[pltpu.VMEM((tm, D), BF16), pltpu.VMEM((tm, LANES), BF16)],
        compiler_params=_params("parallel", "arbitrary"),
        name="gla_in_proj",
    )(x, g, w, w1p, w2p, b_g)


def _gla_chunk_body(q_ref, k_ref, v_ref, gate_ref, g_ref, gn_ref, o_ref, st_ref, *, tb, dk, dv, scale):
    C = GLA_CHUNK

    @pl.when(pl.program_id(1) == 0)
    def _():
        st_ref[...] = jnp.zeros_like(st_ref)

    r = lax.broadcasted_iota(jnp.int32, (tb, tb), 0)
    c = lax.broadcasted_iota(jnp.int32, (tb, tb), 1)
    shift = C.bit_length() - 1
    same = jnp.right_shift(r, shift) == jnp.right_shift(c, shift)
    in_chunk_causal = jnp.logical_and(same, c <= r)
    tril_blk = jnp.where(in_chunk_causal, 1.0, 0.0).astype(BF16)
    b_all = _dot_01(tril_blk, g_ref[...])
    n_chunks = tb // C

    o_intra = []
    for h in range(GLA_HEADS):
        ks_, vs_ = slice(h * dk, (h + 1) * dk), slice(h * dv, (h + 1) * dv)
        b = b_all[:, ks_]
        b_mid = jnp.concatenate([jnp.broadcast_to(b[n * C + C // 2:n * C + C // 2 + 1, :], (C, dk))
                                 for n in range(n_chunks)], axis=0)
        qs = (q_ref[:, ks_].astype(F32) * scale * jnp.exp(b - b_mid)).astype(BF16)
        ks = (k_ref[:, ks_].astype(F32) * jnp.exp(b_mid - b)).astype(BF16)
        scores = jnp.where(in_chunk_causal, _dot_nt(qs, ks), 0.0)
        o_intra.append(_dot(scores.astype(BF16), v_ref[:, vs_]))

    for n in range(n_chunks):
        rows = slice(n * C, (n + 1) * C)
        for h in range(GLA_HEADS):
            ks_, vs_ = slice(h * dk, (h + 1) * dk), slice(h * dv, (h + 1) * dv)
            qc = q_ref[rows, ks_].astype(F32) * scale
            kc = k_ref[rows, ks_].astype(F32)
            vc = v_ref[rows, vs_]
            b = b_all[rows, ks_]
            b_last = b[C - 1:C, :]
            st = st_ref[h]
            o = _dot_nt((qc * jnp.exp(b)).astype(BF16), st.astype(BF16)) + o_intra[h][rows, :]
            k_dec = (kc * jnp.exp(b_last - b)).astype(BF16)
            st_ref[h] = st * jnp.exp(b_last) + _dot_tn(vc, k_dec)
            gated = _rms(o, gn_ref[...]) * _silu(gate_ref[rows, vs_].astype(F32))
            o_ref[rows, vs_] = gated.astype(o_ref.dtype)


def _gla_chunks(u, gdec, gn_g, *, B, S, E, tb=256):
    T = u.shape[0]
    Hh = GLA_HEADS
    KD = gdec.shape[1]
    dk, dv = KD // Hh, E // Hh
    nS = S // tb
    body = functools.partial(_gla_chunk_body, tb=tb, dk=dk, dv=dv, scale=dk ** -0.5)
    return pl.pallas_call(
        body,
        grid=(B, nS),
        in_specs=[
            pl.BlockSpec((tb, KD), lambda b, i: (b * nS + i, 0)),
            pl.BlockSpec((tb, KD), lambda b, i: (b * nS + i, 1)),
            pl.BlockSpec((tb, E), lambda b, i: (b * nS + i, (2 * KD) // E)),
            pl.BlockSpec((tb, E), lambda b, i: (b * nS + i, (2 * KD + E) // E)),
            pl.BlockSpec((tb, KD), lambda b, i: (b * nS + i, 0)),
            pl.BlockSpec((1, dv), lambda b, i: (0, 0)),
        ],
        out_specs=pl.BlockSpec((tb, E), lambda b, i: (b * nS + i, 0)),
        out_shape=jax.ShapeDtypeStruct((T, E), BF16),
        scratch_shapes=[pltpu.VMEM((Hh, dv, dk), F32)],
        compiler_params=_params("parallel", "arbitrary"),
        name="gla_chunks",
    )(u, u, u, u, gdec, gn_g)


def _gla_layer(x, g, w_in, w_g1, w_g2, b_g, gn_g, w_out, *, B, S):
    E = w_out.shape[0]
    KD = w_g2.shape[1]
    rank = w_g1.shape[1]
    w1p = jnp.pad(w_g1, ((0, 0), (0, LANES - rank))).astype(BF16)
    w2p = jnp.pad(w_g2, ((0, LANES - rank), (0, 0))).astype(BF16)
    u, gdec = _gla_in_proj(x, g, w_in.astype(BF16), w1p, w2p, b_g.reshape(1, KD))
    o = _gla_chunks(u, gdec, gn_g.reshape(1, -1), B=B, S=S, E=E)
    return _out_proj(o, w_out.astype(BF16), x)


def _lru_body(xb_ref, xh_ref, gate_ref, x_ref, wc_ref, bc_ref, wax_ref, ba_ref, bx_ref, lam_ref,
              wout_ref, fg_ref, y_ref, buf_ref, a_ref, b_ref, h_ref, carry_ref, *, tm, E):
    i = pl.program_id(1)

    @pl.when(i == 0)
    def _():
        carry_ref[...] = jnp.zeros_like(carry_ref)

    buf_ref[0:LRU_HALO, :] = jnp.where(i > 0, xh_ref[...].astype(F32), 0.0)
    buf_ref[LRU_HALO:, :] = xb_ref[...].astype(F32)
    off = LRU_HALO - (LRU_CONV_K - 1)
    xc = jnp.broadcast_to(bc_ref[...], (tm, E))
    for k in range(LRU_CONV_K):
        xc = xc + wc_ref[k:k + 1, :] * buf_ref[off + k:off + k + tm, :]

    lam = lam_ref[...]
    sp = jnp.maximum(-lam, 0.0) + jnp.log(1.0 + jnp.exp(-jnp.abs(lam)))
    rate = sp * (-LRU_C * 1.4426950408889634)
    W = LRU_BLOCK_W
    for n in range(E // W):
        sl = slice(n * W, (n + 1) * W)
        xn = xc[:, sl]
        pre = _dot(xn.astype(BF16), wax_ref[n])
        r = _sigmoid(pre[:, :W] + ba_ref[:, sl])
        ig = _sigmoid(pre[:, W:] + bx_ref[:, sl])
        a = jnp.exp2(r * rate[:, sl])
        beta = jnp.sqrt(1.0 - a * a)
        a_ref[:, sl] = a
        b_ref[:, sl] = beta * (ig * xn)

    def step(t, h):
        h = a_ref[pl.ds(t, 1), :] * h + b_ref[pl.ds(t, 1), :]
        h_ref[pl.ds(t, 1), :] = h
        return h

    carry_ref[...] = lax.fori_loop(0, tm, step, carry_ref[...], unroll=8)

    og = (h_ref[...] * _silu(gate_ref[...].astype(F32))).astype(BF16)
    y = x_ref[...] + _dot(og, wout_ref[...])
    y_ref[...] = _rms(y, fg_ref[...])


def _lru_layer(x, g, w_in, w_conv, b_conv, w_a, b_a, w_x, b_x, lam, w_out, final_g, *, B, S, tm=256):
    T, D = x.shape
    E = w_out.shape[0]
    u = _rms_matmul(x, g, w_in.astype(BF16))
    wax = jnp.concatenate([w_a, w_x], axis=-1).astype(BF16)
    nS = S // tm
    hpb = tm // LRU_HALO

    def row(n):
        return pl.BlockSpec((1, n), lambda b, i: (0, 0))

    body = functools.partial(_lru_body, tm=tm, E=E)
    return pl.pallas_call(
        body,
        grid=(B, nS),
        in_specs=[
            pl.BlockSpec((tm, E), lambda b, i: (b * nS + i, 0)),
            pl.BlockSpec((LRU_HALO, E), lambda b, i: (jnp.maximum((b * nS + i) * hpb - 1, 0), 0)),
            pl.BlockSpec((tm, E), lambda b, i: (b * nS + i, 1)),
            pl.BlockSpec((tm, D), lambda b, i: (b * nS + i, 0)),
            pl.BlockSpec((LRU_CONV_K, E), lambda b, i: (0, 0)),
            row(E),
            pl.BlockSpec(wax.shape, lambda b, i: (0, 0, 0)),
            row(E), row(E), row(E),
            pl.BlockSpec((E, D), lambda b, i: (0, 0)),
            row(D),
        ],
        out_specs=pl.BlockSpec((tm, D), lambda b, i: (b * nS + i, 0)),
        out_shape=jax.ShapeDtypeStruct((T, D), F32),
        scratch_shapes=[pltpu.VMEM((tm + LRU_HALO, E), F32), pltpu.VMEM((tm, E), F32),
                        pltpu.VMEM((tm, E), F32), pltpu.VMEM((tm, E), F32), pltpu.VMEM((1, E), F32)],
        compiler_params=_params("parallel", "arbitrary"),
        name="rglru_mixer",
    )(u, u, u, x, w_conv, b_conv.reshape(1, E), wax, b_a.reshape(1, E), b_x.reshape(1, E),
      lam.reshape(1, E), w_out.astype(BF16), final_g.reshape(1, D))


def kernel(x, norm_g, final_g, conv_w_in, conv_w_dw, conv_b_dw, conv_ln_g, conv_ln_b, conv_w_out, fox_w_in, fox_b_f, fox_w_out, gla_w_in, gla_w_g1, gla_w_g2, gla_b_g, gla_gn_g, gla_w_out, lru_w_in, lru_w_conv, lru_b_conv, lru_w_a, lru_b_a, lru_w_x, lru_b_x, lru_lam, lru_w_out):
    B, S, D = x.shape
    assert norm_g.shape[0] == 4 and conv_w_in.shape[0] == 1, "one layer per mixer"
    E = conv_w_out.shape[1]
    h = x.reshape(B * S, D)
    ng = norm_g.reshape(4, 1, D)
    h = _conv_layer(h, ng[0], conv_w_in[0].astype(BF16), conv_w_dw[0], conv_b_dw[0].reshape(1, E),
                    conv_ln_g[0].reshape(1, E), conv_ln_b[0].reshape(1, E), conv_w_out[0].astype(BF16), B=B, S=S)
    h = _fox_layer(h, ng[1], fox_w_in[0], fox_b_f[0], fox_w_out[0], B=B, S=S)
    h = _gla_layer(h, ng[2], gla_w_in[0], gla_w_g1[0], gla_w_g2[0], gla_b_g[0], gla_gn_g[0], gla_w_out[0], B=B, S=S)
    h = _lru_layer(h, ng[3], lru_w_in[0], lru_w_conv[0], lru_b_conv[0], lru_w_a[0], lru_b_a[0], lru_w_x[0],
                   lru_b_x[0], lru_lam[0], lru_w_out[0], final_g, B=B, S=S)
    return h.reshape(B, S, D)
```

```python
import functools

import jax
import jax.numpy as jnp
from jax import lax
from jax.experimental import pallas as pl
from jax.experimental.pallas import tpu as pltpu

F32 = jnp.float32
BF16 = jnp.bfloat16
EPS = 1e-6

V7X_VMEM_BYTES = 64 * 1024 * 1024
VMEM_LIMIT_BYTES = V7X_VMEM_BYTES - 16 * 1024 * 1024
LANES = 128
SUBLANES = 8

CONV_K = 31
CONV_HALO = 32
FOX_HEAD_DIM = 128
ONES_ROWS = 16
UNDERFLOW_BITS = 176.0
NORM_SLACK = 1.01
GLA_HEADS = 4
GLA_CHUNK = 64
GLA_GATE_NORM = 16.0
LRU_CONV_K = 4
LRU_HALO = 16
LRU_BLOCK_W = 128
LRU_C = 8.0


def _params(*sem):
    return pltpu.CompilerParams(dimension_semantics=sem, vmem_limit_bytes=VMEM_LIMIT_BYTES)


NEG_LOG2E = -1.4426950408889634


def _exp_neg(x):
    return jnp.exp2(x * NEG_LOG2E)


def _sigmoid(x):
    return 1.0 / (1.0 + _exp_neg(x))


def _silu(x):
    return x * _sigmoid(x)


def _log_sigmoid(x):
    return jnp.minimum(x, 0.0) - jnp.log(1.0 + _exp_neg(jnp.abs(x)))


def _rms(x, g):
    return x * lax.rsqrt(jnp.mean(x * x, axis=-1, keepdims=True) + EPS) * g


def _dot(a, b):
    return jnp.dot(a, b, preferred_element_type=F32)


def _dot_nt(a, b):
    return lax.dot_general(a, b, (((1,), (1,)), ((), ())), preferred_element_type=F32)


def _dot_tn(a, b):
    return lax.dot_general(a, b, (((0,), (0,)), ((), ())), preferred_element_type=F32)


def _dot_01(m01, x):
    x1 = x.astype(BF16)
    r1 = x - x1.astype(F32)
    x2 = r1.astype(BF16)
    x3 = (r1 - x2.astype(F32)).astype(BF16)
    return _dot(m01, x1) + _dot(m01, x2) + _dot(m01, x3)


def _rms_matmul_body(x_ref, g_ref, w_ref, o_ref, hn_ref):
    @pl.when(pl.program_id(1) == 0)
    def _():
        hn_ref[...] = _rms(x_ref[...], g_ref[...]).astype(BF16)

    o_ref[...] = _dot(hn_ref[...], w_ref[...]).astype(o_ref.dtype)


def _rms_matmul(x, g, w, *, tm=1024, tn=2048):
    T, D = x.shape
    N = w.shape[1]
    assert T % tm == 0 and N % tn == 0
    return pl.pallas_call(
        _rms_matmul_body,
        grid=(T // tm, N // tn),
        in_specs=[
            pl.BlockSpec((tm, D), lambda i, j: (i, 0)),
            pl.BlockSpec((1, D), lambda i, j: (0, 0)),
            pl.BlockSpec((D, tn), lambda i, j: (0, j)),
        ],
        out_specs=pl.BlockSpec((tm, tn), lambda i, j: (i, j)),
        out_shape=jax.ShapeDtypeStruct((T, N), BF16),
        scratch_shapes=[pltpu.VMEM((tm, D), BF16)],
        compiler_params=_params("parallel", "arbitrary"),
        name="rms_in_proj",
    )(x, g, w)


def _out_proj_body(og_ref, w_ref, x_ref, y_ref):
    y_ref[...] = x_ref[...] + _dot(og_ref[...], w_ref[...])


def _out_proj(og, w_out, x, *, tm=1024):
    T, E = og.shape
    D = x.shape[1]
    assert T % tm == 0
    return pl.pallas_call(
        _out_proj_body,
        grid=(T // tm,),
        in_specs=[
            pl.BlockSpec((tm, E), lambda i: (i, 0)),
            pl.BlockSpec((E, D), lambda i: (0, 0)),
            pl.BlockSpec((tm, D), lambda i: (i, 0)),
        ],
        out_specs=pl.BlockSpec((tm, D), lambda i: (i, 0)),
        out_shape=jax.ShapeDtypeStruct((T, D), F32),
        compiler_params=_params("parallel"),
        name="out_proj",
    )(og, w_out, x)


def _conv_body(a_ref, b_ref, gate_ref, ah_ref, bh_ref, x_ref, wdw_ref, bdw_ref, lng_ref, lnb_ref,
               wout_ref, y_ref, sh_ref, acc_ref, *, tm, E, rows, cols):
    i = pl.program_id(1)
    off = CONV_HALO - (CONV_K - 1)
    n_sh = tm + CONV_HALO - SUBLANES
    for c0 in range(0, E, cols):
        cs = slice(c0, c0 + cols)
        vh = ah_ref[:, cs].astype(F32) * _sigmoid(bh_ref[:, cs].astype(F32))
        sh_ref[0, 0:CONV_HALO, :] = jnp.where(i > 0, vh, 0.0)
        sh_ref[0, CONV_HALO:, :] = a_ref[:, cs].astype(F32) * _sigmoid(b_ref[:, cs].astype(F32))
        for s in range(1, SUBLANES):
            sh_ref[s, 0:n_sh, :] = sh_ref[0, s:s + n_sh, :]
        for r0 in range(0, tm, rows):
            acc = jnp.broadcast_to(bdw_ref[:, cs], (rows, cols))
            for k in range(CONV_K):
                s, a8 = (off + k) % SUBLANES, (off + k) // SUBLANES * SUBLANES
                wk = jnp.concatenate([wdw_ref[k, :, cs]] * (rows // SUBLANES), axis=0)
                acc = acc + wk * sh_ref[s, r0 + a8:r0 + a8 + rows, :]
            acc_ref[r0:r0 + rows, cs] = acc

    c = acc_ref[...]
    mu = jnp.mean(c, axis=-1, keepdims=True)
    d = c - mu
    var = jnp.mean(d * d, axis=-1, keepdims=True)
    ln = d * lax.rsqrt(var + EPS) * lng_ref[...] + lnb_ref[...]
    og = (_silu(ln) * _silu(gate_ref[...].astype(F32))).astype(BF16)
    y_ref[...] = x_ref[...] + _dot(og, wout_ref[...])


def _conv_layer(x, g, w_in, w_dw, b_dw, ln_g, ln_b, w_out, *, B, S, tm=512):
    T, D = x.shape
    E = w_out.shape[0]
    u = _rms_matmul(x, g, w_in)
    nS = S // tm
    hpb = tm // CONV_HALO

    def tile(col):
        return pl.BlockSpec((tm, E), lambda b, i: (b * nS + i, col))

    def halo(col):
        return pl.BlockSpec((CONV_HALO, E), lambda b, i: (jnp.maximum((b * nS + i) * hpb - 1, 0), col))

    def row(n):
        return pl.BlockSpec((1, n), lambda b, i: (0, 0))

    cols = 512
    body = functools.partial(_conv_body, tm=tm, E=E, rows=32, cols=cols)
    return pl.pallas_call(
        body,
        grid=(B, nS),
        in_specs=[tile(0), tile(1), tile(2), halo(0), halo(1),
                  pl.BlockSpec((tm, D), lambda b, i: (b * nS + i, 0)),
                  pl.BlockSpec((CONV_K, SUBLANES, E), lambda b, i: (0, 0, 0)),
                  row(E), row(E), row(E),
                  pl.BlockSpec((E, D), lambda b, i: (0, 0))],
        out_specs=pl.BlockSpec((tm, D), lambda b, i: (b * nS + i, 0)),
        out_shape=jax.ShapeDtypeStruct((T, D), F32),
        scratch_shapes=[pltpu.VMEM((SUBLANES, tm + CONV_HALO, cols), F32), pltpu.VMEM((tm, E), F32)],
        compiler_params=_params("parallel", "parallel"),
        name="conv_mixer",
    )(u, u, u, u, u, x, jnp.broadcast_to(w_dw[:, None, :], (CONV_K, SUBLANES, E)), b_dw, ln_g, ln_b, w_out)


def _fox_cum_body(x_ref, g_ref, wf_ref, bf_ref, o_ref, carry_ref, *, tc, H):
    @pl.when(pl.program_id(1) == 0)
    def _():
        carry_ref[...] = jnp.zeros_like(carry_ref)

    hn = _rms(x_ref[...], g_ref[...]).astype(BF16)
    log_f = _log_sigmoid(_dot(hn, wf_ref[...]) + bf_ref[...])
    r = lax.broadcasted_iota(jnp.int32, (tc, tc), 0)
    c = lax.broadcasted_iota(jnp.int32, (tc, tc), 1)
    tril = jnp.where(c <= r, 1.0, 0.0).astype(BF16)
    cum = _dot_01(tril, log_f) + carry_ref[...]
    carry_ref[...] = cum[tc - 1:tc, :]
    o_ref[0] = cum.T[0:H, :]


def _fox_cum(x, g, wf, bfp, *, B, S, H, tc=512):
    T, D = x.shape
    nS = S // tc
    body = functools.partial(_fox_cum_body, tc=tc, H=H)
    return pl.pallas_call(
        body,
        grid=(B, nS),
        in_specs=[
            pl.BlockSpec((tc, D), lambda b, i: (b * nS + i, 0)),
            pl.BlockSpec((1, D), lambda b, i: (0, 0)),
            pl.BlockSpec((D, LANES), lambda b, i: (0, 0)),
            pl.BlockSpec((1, LANES), lambda b, i: (0, 0)),
        ],
        out_specs=pl.BlockSpec((1, H, tc), lambda b, i: (b, 0, i)),
        out_shape=jax.ShapeDtypeStruct((B, H, S), F32),
        scratch_shapes=[pltpu.VMEM((1, LANES), F32)],
        compiler_params=_params("parallel", "arbitrary"),
        name="fox_forget_cumsum",
    )(x, g, wf, bfp)


def _fox_attn_body(q_ref, k_ref, v_ref, gate_ref, c_ref, o_ref, vt_ref, ka_ref, cc_ref, s0_ref, s1_ref, qa0_ref, qa1_ref,
                   m_ref, acc_ref, p0_ref, *, S, tq, tk, ch, scale):
    dh = q_ref.shape[1]
    log2e = 1.4426950408889634
    assert tq == 2 * tk, "a query tile spans exactly two kv blocks (pairwise pipelined loop)"
    lane = lax.broadcasted_iota(jnp.int32, (1, LANES), 1)

    def split3(x):
        hi = x.astype(BF16).astype(F32)
        mid = (x - hi).astype(BF16).astype(F32)
        lo = (x - hi - mid).astype(BF16).astype(F32)
        return hi, mid, lo

    ones_sq = jnp.ones((dh, LANES), BF16)

    def max_row_norm2(x):
        rs = _dot((x * x).astype(BF16), ones_sq)
        return jnp.max(rs, axis=0, keepdims=True)

    sel_r = lax.broadcasted_iota(jnp.int32, (SUBLANES, LANES), 0)
    sel_l = lax.broadcasted_iota(jnp.int32, (SUBLANES, LANES), 1)
    sel_sum = jnp.where(sel_r < 3, 1.0, 0.0)
    sel_aug = jnp.where(jnp.logical_and(sel_r < 3, sel_l == sel_r), -1.0,
                        jnp.where(jnp.logical_and(sel_r == 3, jnp.logical_and(sel_l >= 3, sel_l < 6)), 1.0, 0.0))

    kn2 = jnp.zeros((1, LANES), F32)
    for n0 in range(S // ch):
        rows = slice(n0 * ch, (n0 + 1) * ch)
        kn2 = jnp.maximum(kn2, max_row_norm2(k_ref[rows, :].astype(F32)))
        vt = v_ref[rows, :].astype(F32).T.astype(BF16)
        for s0 in range(ch // tk):
            vt_ref[n0 * (ch // tk) + s0, 0:dh, :] = vt[:, s0 * tk:(s0 + 1) * tk]
            vt_ref[n0 * (ch // tk) + s0, dh:, :] = jnp.ones((ONES_ROWS, tk), BF16)
        hi, mid, lo = split3(c_ref[0, 0, n0:n0 + 1, :] * log2e)
        parts = jnp.concatenate([hi, mid, lo, jnp.ones((1, ch), F32), jnp.zeros((SUBLANES - 4, ch), F32)], axis=0)
        cc_ref[rows, :] = _dot_tn(parts, sel_sum)
        ka_ref[rows, 0:dh] = k_ref[rows, :]
        ka_ref[rows, dh:] = _dot_tn(parts, sel_aug).astype(BF16)
    c_end = cc_ref[pl.ds(tk - 1, S // tk, stride=tk), :]

    def scaled_q(i):
        q0 = pl.multiple_of(i * tq, tq)
        return (q_ref[pl.ds(q0, tq), :].astype(F32) * (scale * log2e)).astype(BF16)

    def dead_pairs(i):
        q0 = pl.multiple_of(i * tq, tq)
        qw = scaled_q(i)
        c0 = cc_ref[pl.ds(q0, 1), :]
        qk = jnp.sqrt(max_row_norm2(qw.astype(F32)) * kn2) * NORM_SLACK
        n_dead = jnp.sum(jnp.where(c_end > c0 + (UNDERFLOW_BITS + 2.0 * qk), 1, 0)[:, 0:1])
        return jnp.minimum(n_dead // 2, i)

    def stage(i, qa_ref):
        hi, mid, lo = split3(cc_ref[pl.ds(pl.multiple_of(i * tq, tq), 1), :])
        aug = jnp.where(lane < 3, 1.0, jnp.where(lane == 3, hi, jnp.where(lane == 4, mid,
                        jnp.where(lane == 5, lo, 0.0))))
        qa_ref[:, 0:dh] = scaled_q(i)
        qa_ref[:, dh:] = jnp.broadcast_to(aug, (tq, LANES)).astype(BF16)

    def reset_stats():
        m_ref[...] = jnp.full((1, tq), -jnp.inf, F32)
        acc_ref[...] = jnp.zeros_like(acc_ref)

    def logits(j, s_ref, lo_q, qa_ref):
        kv0 = pl.multiple_of(j * tk, tk)
        s_ref[:, lo_q:] = _dot_nt(ka_ref[pl.ds(kv0, tk), :], qa_ref[lo_q:, :])

    def consume(j, s_ref, diag, lo_q):
        st = s_ref[:, lo_q:]
        if diag is not None:
            r = lax.broadcasted_iota(jnp.int32, st.shape, 0) + diag * tk
            c = lax.broadcasted_iota(jnp.int32, st.shape, 1) + lo_q
            st = jnp.where(r <= c, st, -jnp.inf)
        m_old = m_ref[:, lo_q:]
        m_new = jnp.maximum(m_old, jnp.max(st, axis=0, keepdims=True))
        alpha = jnp.exp2(m_old - m_new)
        p = jnp.exp2(st - m_new)
        m_ref[:, lo_q:] = m_new
        acc_ref[:, lo_q:] = alpha * acc_ref[:, lo_q:] + _dot(vt_ref[j], p.astype(BF16))

    n_tiles = S // tq
    for t in range(n_tiles):
        p0_ref[t] = dead_pairs(t)

    def q_tile(i, qa_ref, qa_next_ref):
        q0 = pl.multiple_of(i * tq, tq)
        p0 = p0_ref[i]

        def pairs(n_pairs, first_pair):
            def body(it, _):
                for u in range(n_pairs):
                    j = 2 * (first_pair + n_pairs * it + u)
                    logits(j + 1, s1_ref, 0, qa_ref)
                    consume(j, s0_ref, None, 0)
                    logits(j + 2, s0_ref, 0, qa_ref)
                    consume(j + 1, s1_ref, None, 0)
                return 0
            return body

        n_pairs = i - p0
        lax.fori_loop(0, n_pairs // 2, pairs(2, p0), 0)
        lax.fori_loop(0, n_pairs % 2, pairs(1, p0 + 2 * (n_pairs // 2)), 0)
        logits(2 * i + 1, s1_ref, tk, qa_ref)
        i_next = jnp.minimum(i + 1, n_tiles - 1)
        stage(i_next, qa_next_ref)
        consume(2 * i, s0_ref, 0, 0)
        logits(2 * p0_ref[i_next], s0_ref, 0, qa_next_ref)
        consume(2 * i + 1, s1_ref, 1, tk)
        o = (acc_ref[0:dh, :] / acc_ref[dh:dh + 1, :]).T
        o_ref[pl.ds(q0, tq), :] = (o * _silu(gate_ref[pl.ds(q0, tq), :].astype(F32))).astype(o_ref.dtype)
        reset_stats()

    def tile_pair(it, _):
        q_tile(2 * it, qa0_ref, qa1_ref)
        q_tile(2 * it + 1, qa1_ref, qa0_ref)
        return 0

    assert n_tiles % 2 == 0, "query tiles alternate between two staging buffers"
    stage(0, qa0_ref)
    reset_stats()
    logits(2 * p0_ref[0], s0_ref, 0, qa0_ref)
    lax.fori_loop(0, n_tiles // 2, tile_pair, 0)


def _fox_attention(u, cum, *, B, S, H, E, tq=1024, tk=512, ch=512):
    T = u.shape[0]
    dh = FOX_HEAD_DIM
    cum4 = cum.reshape(B, H, S // ch, ch)
    body = functools.partial(_fox_attn_body, S=S, tq=tq, tk=tk, ch=ch, scale=dh ** -0.5)
    return pl.pallas_call(
        body,
        grid=(B, H),
        in_specs=[
            pl.BlockSpec((S, dh), lambda b, h: (b, h)),
            pl.BlockSpec((S, dh), lambda b, h: (b, H + h)),
            pl.BlockSpec((S, dh), lambda b, h: (b, 2 * H + h)),
            pl.BlockSpec((S, dh), lambda b, h: (b, 3 * H + h)),
            pl.BlockSpec((1, 1, S // ch, ch), lambda b, h: (b, h, 0, 0)),
        ],
        out_specs=pl.BlockSpec((S, dh), lambda b, h: (b, h)),
        out_shape=jax.ShapeDtypeStruct((T, E), BF16),
        scratch_shapes=[pltpu.VMEM((S // tk, dh + ONES_ROWS, tk), BF16), pltpu.VMEM((S, 2 * dh), BF16),
                        pltpu.VMEM((S, LANES), F32), pltpu.VMEM((tk, tq), F32), pltpu.VMEM((tk, tq), F32),
                        pltpu.VMEM((tq, 2 * dh), BF16), pltpu.VMEM((tq, 2 * dh), BF16), pltpu.VMEM((1, tq), F32),
                        pltpu.VMEM((dh + ONES_ROWS, tq), F32), pltpu.SMEM((S // tq,), jnp.int32)],
        compiler_params=_params("parallel", "parallel"),
        name="fox_attention",
    )(u, u, u, u, cum4)


def _fox_layer(x, g, w_in, b_f, w_out, *, B, S):
    E = w_out.shape[0]
    H = E // FOX_HEAD_DIM
    w_main = w_in[:, :3 * E]
    w_f = w_in[:, 3 * E:3 * E + H]
    w_gate = w_in[:, 3 * E + H:]
    w_cat = jnp.concatenate([w_main, w_gate], axis=1).astype(BF16)
    wf_pad = jnp.pad(w_f, ((0, 0), (0, LANES - H))).astype(BF16)
    bf_pad = jnp.pad(b_f, (0, LANES - H)).reshape(1, LANES)
    u = _rms_matmul(x, g, w_cat)
    cum = _fox_cum(x, g, wf_pad, bf_pad, B=B, S=S, H=H)
    o = _fox_attention(u, cum, B=B, S=S, H=H, E=E)
    return _out_proj(o, w_out.astype(BF16), x)


def _gla_in_proj_body(x_ref, g_ref, w_ref, w1_ref, w2_ref, b_ref, o_ref, d_ref, hn_ref, low_ref):
    @pl.when(pl.program_id(1) == 0)
    def _():
        hn = _rms(x_ref[...], g_ref[...]).astype(BF16)
        hn_ref[...] = hn
        low_ref[...] = _dot(hn, w1_ref[...]).astype(BF16)

    o_ref[...] = _dot(hn_ref[...], w_ref[...]).astype(o_ref.dtype)
    d_ref[...] = _log_sigmoid(_dot(low_ref[...], w2_ref[...]) + b_ref[...]) * (1.0 / GLA_GATE_NORM)


def _gla_in_proj(x, g, w, w1p, w2p, b_g, *, tm=1024, nj=2):
    T, D = x.shape
    N, KD = w.shape[1], w2p.shape[1]
    tn, td = N // nj, KD // nj
    assert T % tm == 0 and tn % LANES == 0 and td % LANES == 0
    return pl.pallas_call(
        _gla_in_proj_body,
        grid=(T // tm, nj),
        in_specs=[
            pl.BlockSpec((tm, D), lambda i, j: (i, 0)),
            pl.BlockSpec((1, D), lambda i, j: (0, 0)),
            pl.BlockSpec((D, tn), lambda i, j: (0, j)),
            pl.BlockSpec((D, LANES), lambda i, j: (0, 0)),
            pl.BlockSpec((LANES, td), lambda i, j: (0, j)),
            pl.BlockSpec((1, td), lambda i, j: (0, j)),
        ],
        out_specs=[pl.BlockSpec((tm, tn), lambda i, j: (i, j)), pl.BlockSpec((tm, td), lambda i, j: (i, j))],
        out_shape=[jax.ShapeDtypeStruct((T, N), BF16), jax.ShapeDtypeStruct((T, KD), F32)],
        scratch_shapes=[pltpu.VMEM((tm, D), BF16), pltpu.VMEM((tm, LANES), BF16)],
        compiler_params=_params("parallel", "arbitrary"),
        name="gla_in_proj",
    )(x, g, w, w1p, w2p, b_g)


def _gla_chunk_body(q_ref, k_ref, v_ref, gate_ref, g_ref, gn_ref, o_ref, st_ref, *, tb, dk, dv, scale):
    C = GLA_CHUNK

    @pl.when(pl.program_id(1) == 0)
    def _():
        st_ref[...] = jnp.zeros_like(st_ref)

    r = lax.broadcasted_iota(jnp.int32, (tb, tb), 0)
    c = lax.broadcasted_iota(jnp.int32, (tb, tb), 1)
    shift = C.bit_length() - 1
    same = jnp.right_shift(r, shift) == jnp.right_shift(c, shift)
    in_chunk_causal = jnp.logical_and(same, c <= r)
    tril_blk = jnp.where(in_chunk_causal, 1.0, 0.0).astype(BF16)
    b_all = _dot_01(tril_blk, g_ref[...])
    n_chunks = tb // C

    o_intra = []
    for h in range(GLA_HEADS):
        ks_, vs_ = slice(h * dk, (h + 1) * dk), slice(h * dv, (h + 1) * dv)
        b = b_all[:, ks_]
        b_mid = jnp.concatenate([jnp.broadcast_to(b[n * C + C // 2:n * C + C // 2 + 1, :], (C, dk))
                                 for n in range(n_chunks)], axis=0)
        qs = (q_ref[:, ks_].astype(F32) * scale * jnp.exp(b - b_mid)).astype(BF16)
        ks = (k_ref[:, ks_].astype(F32) * jnp.exp(b_mid - b)).astype(BF16)
        scores = jnp.where(in_chunk_causal, _dot_nt(qs, ks), 0.0)
        o_intra.append(_dot(scores.astype(BF16), v_ref[:, vs_]))

    for n in range(n_chunks):
        rows = slice(n * C, (n + 1) * C)
        for h in range(GLA_HEADS):
            ks_, vs_ = slice(h * dk, (h + 1) * dk), slice(h * dv, (h + 1) * dv)
            qc = q_ref[rows, ks_].astype(F32) * scale
            kc = k_ref[rows, ks_].astype(F32)
            vc = v_ref[rows, vs_]
            b = b_all[rows, ks_]
            b_last = b[C - 1:C, :]
            st = st_ref[h]
            o = _dot_nt((qc * jnp.exp(b)).astype(BF16), st.astype(BF16)) + o_intra[h][rows, :]
            k_dec = (kc * jnp.exp(b_last - b)).astype(BF16)
            st_ref[h] = st * jnp.exp(b_last) + _dot_tn(vc, k_dec)
            gated = _rms(o, gn_ref[...]) * _silu(gate_ref[rows, vs_].astype(F32))
            o_ref[rows, vs_] = gated.astype(o_ref.dtype)


def _gla_chunks(u, gdec, gn_g, *, B, S, E, tb=256):
    T = u.shape[0]
    Hh = GLA_HEADS
    KD = gdec.shape[1]
    dk, dv = KD // Hh, E // Hh
    nS = S // tb
    body = functools.partial(_gla_chunk_body, tb=tb, dk=dk, dv=dv, scale=dk ** -0.5)
    return pl.pallas_call(
        body,
        grid=(B, nS),
        in_specs=[
            pl.BlockSpec((tb, KD), lambda b, i: (b * nS + i, 0)),
            pl.BlockSpec((tb, KD), lambda b, i: (b * nS + i, 1)),
            pl.BlockSpec((tb, E), lambda b, i: (b * nS + i, (2 * KD) // E)),
            pl.BlockSpec((tb, E), lambda b, i: (b * nS + i, (2 * KD + E) // E)),
            pl.BlockSpec((tb, KD), lambda b, i: (b * nS + i, 0)),
            pl.BlockSpec((1, dv), lambda b, i: (0, 0)),
        ],
        out_specs=pl.BlockSpec((tb, E), lambda b, i: (b * nS + i, 0)),
        out_shape=jax.ShapeDtypeStruct((T, E), BF16),
        scratch_shapes=[pltpu.VMEM((Hh, dv, dk), F32)],
        compiler_params=_params("parallel", "arbitrary"),
        name="gla_chunks",
    )(u, u, u, u, gdec, gn_g)


def _gla_layer(x, g, w_in, w_g1, w_g2, b_g, gn_g, w_out, *, B, S):
    E = w_out.shape[0]
    KD = w_g2.shape[1]
    rank = w_g1.shape[1]
    w1p = jnp.pad(w_g1, ((0, 0), (0, LANES - rank))).astype(BF16)
    w2p = jnp.pad(w_g2, ((0, LANES - rank), (0, 0))).astype(BF16)
    u, gdec = _gla_in_proj(x, g, w_in.astype(BF16), w1p, w2p, b_g.reshape(1, KD))
    o = _gla_chunks(u, gdec, gn_g.reshape(1, -1), B=B, S=S, E=E)
    return _out_proj(o, w_out.astype(BF16), x)


def _lru_body(xb_ref, xh_ref, gate_ref, x_ref, wc_ref, bc_ref, wax_ref, ba_ref, bx_ref, lam_ref,
              wout_ref, fg_ref, y_ref, buf_ref, a_ref, b_ref, h_ref, carry_ref, *, tm, E):
    i = pl.program_id(1)

    @pl.when(i == 0)
    def _():
        carry_ref[...] = jnp.zeros_like(carry_ref)

    buf_ref[0:LRU_HALO, :] = jnp.where(i > 0, xh_ref[...].astype(F32), 0.0)
    buf_ref[LRU_HALO:, :] = xb_ref[...].astype(F32)
    off = LRU_HALO - (LRU_CONV_K - 1)
    xc = jnp.broadcast_to(bc_ref[...], (tm, E))
    for k in range(LRU_CONV_K):
        xc = xc + wc_ref[k:k + 1, :] * buf_ref[off + k:off + k + tm, :]

    lam = lam_ref[...]
    sp = jnp.maximum(-lam, 0.0) + jnp.log(1.0 + jnp.exp(-jnp.abs(lam)))
    rate = sp * (-LRU_C * 1.4426950408889634)
    W = LRU_BLOCK_W
    for n in range(E // W):
        sl = slice(n * W, (n + 1) * W)
        xn = xc[:, sl]
        pre = _dot(xn.astype(BF16), wax_ref[n])
        r = _sigmoid(pre[:, :W] + ba_ref[:, sl])
        ig = _sigmoid(pre[:, W:] + bx_ref[:, sl])
        a = jnp.exp2(r * rate[:, sl])
        beta = jnp.sqrt(1.0 - a * a)
        a_ref[:, sl] = a
        b_ref[:, sl] = beta * (ig * xn)

    def step(t, h):
        h = a_ref[pl.ds(t, 1), :] * h + b_ref[pl.ds(t, 1), :]
        h_ref[pl.ds(t, 1), :] = h
        return h

    carry_ref[...] = lax.fori_loop(0, tm, step, carry_ref[...], unroll=8)

    og = (h_ref[...] * _silu(gate_ref[...].astype(F32))).astype(BF16)
    y = x_ref[...] + _dot(og, wout_ref[...])
    y_ref[...] = _rms(y, fg_ref[...])


def _lru_layer(x, g, w_in, w_conv, b_conv, w_a, b_a, w_x, b_x, lam, w_out, final_g, *, B, S, tm=512):
    T, D = x.shape
    E = w_out.shape[0]
    u = _rms_matmul(x, g, w_in.astype(BF16))
    wax = jnp.concatenate([w_a, w_x], axis=-1).astype(BF16)
    nS = S // tm
    hpb = tm // LRU_HALO

    def row(n):
        return pl.BlockSpec((1, n), lambda b, i: (0, 0))

    body = functools.partial(_lru_body, tm=tm, E=E)
    return pl.pallas_call(
        body,
        grid=(B, nS),
        in_specs=[
            pl.BlockSpec((tm, E), lambda b, i: (b * nS + i, 0)),
            pl.BlockSpec((LRU_HALO, E), lambda b, i: (jnp.maximum((b * nS + i) * hpb - 1, 0), 0)),
            pl.BlockSpec((tm, E), lambda b, i: (b * nS + i, 1)),
            pl.BlockSpec((tm, D), lambda b, i: (b * nS + i, 0)),
            pl.BlockSpec((LRU_CONV_K, E), lambda b, i: (0, 0)),
            row(E),
            pl.BlockSpec(wax.shape, lambda b, i: (0, 0, 0)),
            row(E), row(E), row(E),
            pl.BlockSpec((E, D), lambda b, i: (0, 0)),
            row(D),
        ],
        out_specs=pl.BlockSpec((tm, D), lambda b, i: (b * nS + i, 0)),
        out_shape=jax.ShapeDtypeStruct((T, D), F32),
        scratch_shapes=[pltpu.VMEM((tm + LRU_HALO, E), F32), pltpu.VMEM((tm, E), F32),
                        pltpu.VMEM((tm, E), F32), pltpu.VMEM((tm, E), F32), pltpu.VMEM((1, E), F32)],
        compiler_params=_params("parallel", "arbitrary"),
        name="rglru_mixer",
    )(u, u, u, x, w_conv, b_conv.reshape(1, E), wax, b_a.reshape(1, E), b_x.reshape(1, E),
      lam.reshape(1, E), w_out.astype(BF16), final_g.reshape(1, D))


def kernel(x, norm_g, final_g, conv_w_in, conv_w_dw, conv_b_dw, conv_ln_g, conv_ln_b, conv_w_out, fox_w_in, fox_b_f, fox_w_out, gla_w_in, gla_w_g1, gla_w_g2, gla_b_g, gla_gn_g, gla_w_out, lru_w_in, lru_w_conv, lru_b_conv, lru_w_a, lru_b_a, lru_w_x, lru_b_x, lru_lam, lru_w_out):
    B, S, D = x.shape
    assert norm_g.shape[0] == 4 and conv_w_in.shape[0] == 1, "one layer per mixer"
    E = conv_w_out.shape[1]
    h = x.reshape(B * S, D)
    ng = norm_g.reshape(4, 1, D)
    h = _conv_layer(h, ng[0], conv_w_in[0].astype(BF16), conv_w_dw[0], conv_b_dw[0].reshape(1, E),
                    conv_ln_g[0].reshape(1, E), conv_ln_b[0].reshape(1, E), conv_w_out[0].astype(BF16), B=B, S=S)
    h = _fox_layer(h, ng[1], fox_w_in[0], fox_b_f[0], fox_w_out[0], B=B, S=S)
    h = _gla_layer(h, ng[2], gla_w_in[0], gla_w_g1[0], gla_w_g2[0], gla_b_g[0], gla_gn_g[0], gla_w_out[0], B=B, S=S)
    h = _lru_layer(h, ng[3], lru_w_in[0], lru_w_conv[0], lru_b_conv[0], lru_w_a[0], lru_b_a[0], lru_w_x[0],
                   lru_b_x[0], lru_lam[0], lru_w_out[0], final_g, B=B, S=S)
    return h.reshape(B, S, D)
```
